```python
import jax, jax.numpy as jnp
from jax import lax
import numpy as np

D_MODEL = 1024
BATCH = 32
SEQ = 2048
DEPTH = 4

GRID_W = 64
CTX_LEN = 256
N_MIXERS = 3
NORM_EPS = 1e-6

A_WIDTH = 2 * D_MODEL
A_CHUNK = 128
A_GROUPS = 8
B_WIDTH = 2 * D_MODEL
B_WINDOWS = (2, 4, 8, 16)
B_GROUP_W = B_WIDTH // len(B_WINDOWS)
C_HEAD_DIM = 64
C_Q_HEADS = D_MODEL // C_HEAD_DIM
C_KV_HEADS = C_Q_HEADS // 4
C_GROUP = C_Q_HEADS // C_KV_HEADS
C_QW = C_Q_HEADS * C_HEAD_DIM
C_KVW = C_KV_HEADS * C_HEAD_DIM
C_WINDOW = 128
C_BLOCK = 128
ROPE_THETA = 10000.0

kernel_name = 'hybrid_gmlp_pool_swa_prefix_dit'


def _rmsnorm(x, g):
    xf = x.astype(jnp.float32)
    y = xf * lax.rsqrt(jnp.mean(xf * xf, axis=-1, keepdims=True) + NORM_EPS)
    return (y * g.astype(jnp.float32)).astype(x.dtype)


def _layernorm(x, g, b):
    xf = x.astype(jnp.float32)
    mu = jnp.mean(xf, axis=-1, keepdims=True)
    var = jnp.mean(jnp.square(xf - mu), axis=-1, keepdims=True)
    y = (xf - mu) * lax.rsqrt(var + NORM_EPS) * g.astype(jnp.float32) + b.astype(jnp.float32)
    return y.astype(x.dtype)


def _chunk_gmlp(h, w_in, vn_g, vn_b, w_s, b_s, w_out):
    bn, L, _ = h.shape
    u, v, gate = jnp.split(h @ w_in, 3, axis=-1)
    u = jax.nn.gelu(u)
    v = _layernorm(jax.nn.gelu(v), vn_g, vn_b)
    vc = v.reshape(bn, L // A_CHUNK, A_CHUNK, A_GROUPS, A_WIDTH // A_GROUPS)
    s = jnp.einsum('gpq,bnqgc->bnpgc', w_s, vc) + b_s.T[None, None, :, :, None]
    y = u * s.reshape(bn, L, A_WIDTH)
    return (y * jax.nn.silu(gate)) @ w_out


def _centred_mean(p, w):
    bn, L, ch = p.shape
    cs = jnp.concatenate([jnp.zeros((bn, 1, ch), jnp.float32),
                          jnp.cumsum(p.astype(jnp.float32), axis=1)], axis=1)
    t = jnp.arange(L)
    lo = jnp.clip(t - w // 2, 0, L)
    hi = jnp.clip(t + w // 2, 0, L)
    cnt = (hi - lo).astype(jnp.float32)
    return ((cs[:, hi] - cs[:, lo]) / cnt[None, :, None]).astype(p.dtype)


def _multiscale_pool(h, w_in, w_grp, scale, w_out):
    p, gate = jnp.split(h @ w_in, 2, axis=-1)
    groups = jnp.split(p, len(B_WINDOWS), axis=-1)
    pooled = jnp.stack([_centred_mean(g, w) - g for g, w in zip(groups, B_WINDOWS)], axis=2)
    mixed = jnp.einsum('blgc,gcd->blgd', pooled, w_grp).reshape(h.shape[0], h.shape[1], B_WIDTH) * scale
    return (mixed * jax.nn.silu(gate)) @ w_out


def _axial_rope_tables(rows):
    row = jnp.repeat(jnp.arange(rows), GRID_W).astype(jnp.float32)
    col = jnp.tile(jnp.arange(GRID_W), rows).astype(jnp.float32)
    n_freq = C_HEAD_DIM // 4
    inv = ROPE_THETA ** (-jnp.arange(n_freq, dtype=jnp.float32) / n_freq)
    ang = jnp.concatenate([row[:, None] * inv, col[:, None] * inv], axis=-1)
    return jnp.cos(ang), jnp.sin(ang)


def _apply_rope(x, cos, sin):
    shp = (1, cos.shape[0]) + (1,) * (x.ndim - 3) + (cos.shape[1],)
    c = cos.reshape(shp)
    s = sin.reshape(shp)
    x1, x2 = jnp.split(x.astype(jnp.float32), 2, axis=-1)
    return jnp.concatenate([x1 * c - x2 * s, x2 * c + x1 * s], axis=-1).astype(x.dtype)


def _heads_q(t):
    return t.reshape(t.shape[:2] + (C_KV_HEADS, C_GROUP, C_HEAD_DIM))


def _heads_kv(t):
    return t.reshape(t.shape[:2] + (C_KV_HEADS, C_HEAD_DIM))


def _banded_attention(q, k, v, kc, vc, sink):
    bn, S = q.shape[:2]
    lc = kc.shape[1]
    nb = S // C_BLOCK
    span = C_BLOCK + 2 * C_WINDOW
    pad = ((0, 0), (C_WINDOW, C_WINDOW), (0, 0), (0, 0))
    kp = jnp.pad(k, pad)
    vp = jnp.pad(v, pad)
    scale = C_HEAD_DIM ** -0.5
    rel = jnp.arange(C_BLOCK)[:, None] + C_WINDOW - jnp.arange(span)[None, :]
    band = jnp.abs(rel) <= C_WINDOW
    sink_b = sink.astype(jnp.float32).reshape(C_KV_HEADS, C_GROUP)[None, :, :, None, None]
    neg = jnp.finfo(jnp.float32).min

    def one_block(n):
        start = n * C_BLOCK
        qb = lax.dynamic_slice_in_dim(q, start, C_BLOCK, axis=1)
        kb = lax.dynamic_slice_in_dim(kp, start, span, axis=1)
        vb = lax.dynamic_slice_in_dim(vp, start, span, axis=1)
        kpos = start - C_WINDOW + jnp.arange(span)
        valid = band & ((kpos >= 0) & (kpos < S))[None, :]
        s_loc = jnp.einsum('bqkgd,bskd->bkgqs', qb, kb).astype(jnp.float32) * scale
        s_loc = jnp.where(valid, s_loc, neg)
        s_ctx = jnp.einsum('bqkgd,bskd->bkgqs', qb, kc).astype(jnp.float32) * scale
        s_snk = jnp.broadcast_to(sink_b, s_loc.shape[:-1] + (1,))
        p = jax.nn.softmax(jnp.concatenate([s_loc, s_ctx, s_snk], axis=-1), axis=-1)
        p_loc = p[..., :span].astype(v.dtype)
        p_ctx = p[..., span:span + lc].astype(v.dtype)
        return (jnp.einsum('bkgqs,bskd->bqkgd', p_loc, vb)
                + jnp.einsum('bkgqs,bskd->bqkgd', p_ctx, vc))

    outs = lax.map(one_block, jnp.arange(nb))
    return jnp.moveaxis(outs, 0, 1).reshape(bn, S, C_QW)


def _context_attention(qc, kc, vc, sink):
    scale = C_HEAD_DIM ** -0.5
    s = jnp.einsum('bqkgd,bskd->bkgqs', qc, kc).astype(jnp.float32) * scale
    sink_b = sink.astype(jnp.float32).reshape(C_KV_HEADS, C_GROUP)[None, :, :, None, None]
    s = jnp.concatenate([s, jnp.broadcast_to(sink_b, s.shape[:-1] + (1,))], axis=-1)
    p = jax.nn.softmax(s, axis=-1)[..., :-1].astype(vc.dtype)
    o = jnp.einsum('bkgqs,bskd->bqkgd', p, vc)
    return o.reshape(qc.shape[0], qc.shape[1], C_QW)


def _n_of_kind(kind):
    return len(range(kind, DEPTH, N_MIXERS))


def setup_inputs(seed: int = 0) -> dict:
    key = jax.random.key(seed)
    ks = iter(jax.random.split(key, 32))
    nrm = lambda shape, s: jax.random.normal(next(ks), shape, jnp.float32) * s
    nA, nB, nC = _n_of_kind(0), _n_of_kind(1), _n_of_kind(2)
    D = D_MODEL
    return {
        'x': nrm((BATCH, SEQ, D), 1.0),
        'c': nrm((BATCH, D), 1.0),
        'ctx': nrm((BATCH, CTX_LEN, D), 1.0),
        'c_ctx': nrm((D,), 1.0),
        'ada_w': nrm((DEPTH, D, 3 * D), 0.5 * D ** -0.5),
        'ada_b': nrm((DEPTH, 3 * D), 0.02),
        'norm_g': 1.0 + nrm((DEPTH, D), 0.02),
        'final_g': 1.0 + nrm((D,), 0.02),
        'gmlp_w_in': nrm((nA, D, 3 * A_WIDTH), D ** -0.5),
        'gmlp_vnorm_g': 1.0 + nrm((nA, A_WIDTH), 0.02),
        'gmlp_vnorm_b': nrm((nA, A_WIDTH), 0.02),
        'gmlp_w_s': nrm((nA, A_GROUPS, A_CHUNK, A_CHUNK), A_CHUNK ** -0.5),
        'gmlp_b_s': 1.0 + nrm((nA, A_GROUPS, A_CHUNK), 0.02),
        'gmlp_w_out': nrm((nA, A_WIDTH, D), A_WIDTH ** -0.5),
        'pool_w_in': nrm((nB, D, 2 * B_WIDTH), D ** -0.5),
        'pool_w_grp': nrm((nB, len(B_WINDOWS), B_GROUP_W, B_GROUP_W), B_GROUP_W ** -0.5),
        'pool_scale': 1.0 + nrm((nB, B_WIDTH), 0.1),
        'pool_w_out': nrm((nB, B_WIDTH, D), B_WIDTH ** -0.5),
        'attn_w_in': nrm((nC, D, 2 * C_QW + 2 * C_KVW), D ** -0.5),
        'attn_sink': nrm((nC, C_Q_HEADS), 1.0),
        'attn_w_out': nrm((nC, C_QW, D), C_QW ** -0.5),
    }


def reference(x, c, ctx, c_ctx, ada_w, ada_b, norm_g, final_g,
              gmlp_w_in, gmlp_vnorm_g, gmlp_vnorm_b, gmlp_w_s, gmlp_b_s, gmlp_w_out,
              pool_w_in, pool_w_grp, pool_scale, pool_w_out,
              attn_w_in, attn_sink, attn_w_out):
    ROWS = x.shape[1] // GRID_W
    cos, sin = _axial_rope_tables(ROWS)
    cond_lat = jax.nn.silu(c.astype(jnp.float32))
    cond_ctx = jax.nn.silu(c_ctx.astype(jnp.float32))
    for i in range(DEPTH):
        kind = i % N_MIXERS
        j = i // N_MIXERS
        ctx_out = any(l % N_MIXERS == 2 for l in range(i + 1, DEPTH))
        ctx_in = ctx_out or kind == 2

        shift, scale, gate = jnp.split((cond_lat @ ada_w[i] + ada_b[i]).astype(x.dtype), 3, axis=-1)
        h = _rmsnorm(x, norm_g[i]) * (1 + scale[:, None]) + shift[:, None]
        if ctx_in:
            shift_c, scale_c, gate_c = jnp.split((cond_ctx @ ada_w[i] + ada_b[i]).astype(ctx.dtype), 3, axis=-1)
            hc = _rmsnorm(ctx, norm_g[i]) * (1 + scale_c) + shift_c

        if kind == 0:
            args = (gmlp_w_in[j], gmlp_vnorm_g[j], gmlp_vnorm_b[j], gmlp_w_s[j], gmlp_b_s[j], gmlp_w_out[j])
            y = _chunk_gmlp(h, *args)
            if ctx_out:
                yc = _chunk_gmlp(hc, *args)
        elif kind == 1:
            args = (pool_w_in[j], pool_w_grp[j], pool_scale[j], pool_w_out[j])
            y = _multiscale_pool(h, *args)
            if ctx_out:
                yc = _multiscale_pool(hc, *args)
        else:
            w_in = attn_w_in[j]
            q, k, v, g_att = jnp.split(h @ w_in, [C_QW, C_QW + C_KVW, C_QW + 2 * C_KVW], axis=-1)
            q = _apply_rope(_heads_q(q), cos, sin)
            k = _apply_rope(_heads_kv(k), cos, sin)
            v = _heads_kv(v)
            kc, vc = jnp.split(hc @ w_in[:, C_QW:C_QW + 2 * C_KVW], 2, axis=-1)
            kc = _heads_kv(kc)
            vc = _heads_kv(vc)
            o = _banded_attention(q, k, v, kc, vc, attn_sink[j])
            y = (o * jax.nn.silu(g_att)) @ attn_w_out[j]
            if ctx_out:
                qc = _heads_q(hc @ w_in[:, :C_QW])
                gc = hc @ w_in[:, C_QW + 2 * C_KVW:]
                yc = (_context_attention(qc, kc, vc, attn_sink[j]) * jax.nn.silu(gc)) @ attn_w_out[j]

        x = x + gate[:, None] * y
        if ctx_out:
            ctx = ctx + gate_c * yc
    return _rmsnorm(x, final_g)
```

```python
import functools

import jax
import jax.numpy as jnp
from jax import lax
from jax.experimental import pallas as pl
from jax.experimental.pallas import tpu as pltpu

NORM_EPS = 1e-6
GRID_W = 64
A_CHUNK = 128
B_WINDOWS = (2, 4, 8, 16)
B_HALO = 8
HEAD_DIM = 64
KV_HEADS = 4
Q_GROUP = 4
ATT_WINDOW = 128
ATT_BLOCK = 128
ROPE_THETA = 10000.0
LANES = 128
COND_ROWS_ALIGN = 8

F32 = jnp.float32
BF16 = jnp.bfloat16

VMEM_LIMIT_BYTES = 56 * 1024 * 1024


def _const_spec(shape):
    zeros = (0,) * len(shape)
    return pl.BlockSpec(shape, lambda *_: zeros, pipeline_mode=pl.Buffered(1))


def _params(n_axes):
    return pltpu.CompilerParams(
        dimension_semantics=("parallel",) * n_axes,
        vmem_limit_bytes=VMEM_LIMIT_BYTES,
    )


def _rms(x, g):
    return x * lax.rsqrt(jnp.mean(x * x, axis=-1, keepdims=True) + NORM_EPS) * g


def _split_mod(mod, d):
    return mod[:, :d], mod[:, d:2 * d], mod[:, 2 * d:]


def _bdot(a, b):
    return jnp.dot(a, b, preferred_element_type=F32)


def _mod_kernel(c_ref, w_ref, b_ref, o_ref):
    cond = jax.nn.silu(c_ref[...])
    o_ref[0] = jnp.dot(cond, w_ref[0], preferred_element_type=F32,
                       precision=lax.Precision.HIGHEST) + b_ref[0]


def _modulation(c_all, ada_w, ada_b):
    depth, d, d3 = ada_w.shape
    rows = c_all.shape[0]
    tn = d
    return pl.pallas_call(
        _mod_kernel,
        grid=(depth, d3 // tn),
        in_specs=[
            pl.BlockSpec((rows, d), lambda i, j: (0, 0)),
            pl.BlockSpec((1, d, tn), lambda i, j: (i, 0, j)),
            pl.BlockSpec((1, 1, tn), lambda i, j: (i, 0, j)),
        ],
        out_specs=pl.BlockSpec((1, rows, tn), lambda i, j: (i, 0, j)),
        out_shape=jax.ShapeDtypeStruct((depth, rows, d3), F32),
        compiler_params=_params(2),
        name="adaln_modulation",
    )(c_all, ada_w, ada_b.reshape(depth, 1, d3))


def _mod_spec(mod):
    if mod.shape[0] == 1:
        return pl.BlockSpec((1, 1, mod.shape[2]), lambda b, i: (0, 0, 0))
    return pl.BlockSpec((1, 1, mod.shape[2]), lambda b, i: (b, 0, 0))


def _gmlp_kernel(x_ref, mod_ref, ng_ref, win_ref, vg_ref, vb_ref, ws_ref, bs_ref, wout_ref, fg_ref,
                 o_ref, v_scr, y_scr, *, final_norm):
    tm, d = x_ref.shape[1], x_ref.shape[2]
    width = wout_ref.shape[0]
    groups = ws_ref.shape[0]
    gw = width // groups
    x = x_ref[0]
    shift, scale, gate = _split_mod(mod_ref[0], d)
    hb = (_rms(x, ng_ref[...]) * (1.0 + scale) + shift).astype(BF16)

    v_scr[...] = jax.nn.gelu(_bdot(hb, win_ref[:, width:2 * width]))
    v = v_scr[...]
    mu = jnp.mean(v, axis=-1, keepdims=True)
    var = jnp.mean(jnp.square(v - mu), axis=-1, keepdims=True)
    rstd = lax.rsqrt(var + NORM_EPS)

    for g in range(groups):
        lo = g * gw
        vn = ((v_scr[:, lo:lo + gw] - mu) * rstd * vg_ref[:, lo:lo + gw]
              + vb_ref[:, lo:lo + gw]).astype(BF16)
        bias = bs_ref[:, g:g + 1]
        s = jnp.concatenate(
            [_bdot(ws_ref[g], vn[c * A_CHUNK:(c + 1) * A_CHUNK]) + bias
             for c in range(tm // A_CHUNK)], axis=0)
        u = jax.nn.gelu(_bdot(hb, win_ref[:, lo:lo + gw]))
        gt = _bdot(hb, win_ref[:, 2 * width + lo:2 * width + lo + gw])
        y_scr[:, lo:lo + gw] = ((u * s) * jax.nn.silu(gt)).astype(BF16)

    out = x + gate * _bdot(y_scr[...], wout_ref[...])
    if final_norm:
        out = _rms(out, fg_ref[...])
    o_ref[0] = out


def _gmlp_layer(x, mod, ng, w_in, vn_g, vn_b, w_s, b_s, w_out, final_g, *, final_norm, tm):
    bn, seq, d = x.shape
    width = w_out.shape[0]
    groups = w_s.shape[0]
    return pl.pallas_call(
        functools.partial(_gmlp_kernel, final_norm=final_norm),
        grid=(bn, seq // tm),
        in_specs=[
            pl.BlockSpec((1, tm, d), lambda b, i: (b, i, 0)),
            _mod_spec(mod),
            _const_spec((1, d)),
            _const_spec((d, 3 * width)),
            _const_spec((1, width)),
            _const_spec((1, width)),
            _const_spec((groups, A_CHUNK, A_CHUNK)),
            _const_spec((A_CHUNK, groups)),
            _const_spec((width, d)),
            _const_spec((1, d)),
        ],
        out_specs=pl.BlockSpec((1, tm, d), lambda b, i: (b, i, 0)),
        out_shape=jax.ShapeDtypeStruct(x.shape, F32),
        scratch_shapes=[pltpu.VMEM((tm, width), F32), pltpu.VMEM((tm, width), BF16)],
        compiler_params=_params(2),
        name="gmlp_layer",
    )(x, mod, ng.reshape(1, d), w_in.astype(BF16), vn_g.reshape(1, width), vn_b.reshape(1, width),
      w_s.astype(BF16), b_s.T, w_out.astype(BF16), final_g.reshape(1, d))


def _pool_kernel(xm_ref, xp_ref, xn_ref, mod_ref, ng_ref, win_ref, wgrp_ref, sc_ref, wout_ref,
                 o_ref, p_scr, y_scr, *, seq):
    tm, d = xm_ref.shape[1], xm_ref.shape[2]
    width = wout_ref.shape[0]
    gw = width // len(B_WINDOWS)
    i = pl.program_id(1)
    x = xm_ref[0]
    shift, scale, gate = _split_mod(mod_ref[0], d)
    xe = jnp.concatenate([xp_ref[0], x, xn_ref[0]], axis=0)
    he = _rms(xe, ng_ref[...]) * (1.0 + scale) + shift

    pos_e = i * tm - B_HALO + lax.broadcasted_iota(jnp.int32, (tm + 2 * B_HALO, 1), 0)
    in_seq = (pos_e >= 0) & (pos_e < seq)
    p_scr[...] = jnp.where(in_seq, _bdot(he.astype(BF16), win_ref[:, :width]), 0.0)
    gt = _bdot(he[B_HALO:B_HALO + tm].astype(BF16), win_ref[:, width:])

    pos = i * tm + lax.broadcasted_iota(jnp.int32, (tm, 1), 0)
    for g, w in enumerate(B_WINDOWS):
        lo = g * gw
        acc = p_scr[B_HALO - w // 2:B_HALO - w // 2 + tm, lo:lo + gw]
        for k in range(-w // 2 + 1, w // 2):
            acc = acc + p_scr[B_HALO + k:B_HALO + k + tm, lo:lo + gw]
        cnt = jnp.minimum(pos + w // 2, seq) - jnp.maximum(pos - w // 2, 0)
        pooled = acc * (1.0 / cnt.astype(F32)) - p_scr[B_HALO:B_HALO + tm, lo:lo + gw]
        mixed = _bdot(pooled.astype(BF16), wgrp_ref[g]) * sc_ref[:, lo:lo + gw]
        y_scr[:, lo:lo + gw] = (mixed * jax.nn.silu(gt[:, lo:lo + gw])).astype(BF16)

    o_ref[0] = x + gate * _bdot(y_scr[...], wout_ref[...])


def _pool_layer(x, mod, ng, w_in, w_grp, scale, w_out, *, tm):
    bn, seq, d = x.shape
    width = w_out.shape[0]
    n_grp, gw, _ = w_grp.shape
    per_tile = tm // B_HALO
    n_halo_blocks = seq // B_HALO
    return pl.pallas_call(
        functools.partial(_pool_kernel, seq=seq),
        grid=(bn, seq // tm),
        in_specs=[
            pl.BlockSpec((1, tm, d), lambda b, i: (b, i, 0)),
            pl.BlockSpec((1, B_HALO, d), lambda b, i: (b, jnp.maximum(i * per_tile - 1, 0), 0)),
            pl.BlockSpec((1, B_HALO, d),
                         lambda b, i: (b, jnp.minimum((i + 1) * per_tile, n_halo_blocks - 1), 0)),
            _mod_spec(mod),
            _const_spec((1, d)),
            _const_spec((d, 2 * width)),
            _const_spec((n_grp, gw, gw)),
            _const_spec((1, width)),
            _const_spec((width, d)),
        ],
        out_specs=pl.BlockSpec((1, tm, d), lambda b, i: (b, i, 0)),
        out_shape=jax.ShapeDtypeStruct(x.shape, F32),
        scratch_shapes=[pltpu.VMEM((tm + 2 * B_HALO, width), F32), pltpu.VMEM((tm, width), BF16)],
        compiler_params=_params(2),
        name="pool_layer",
    )(x, x, x, mod, ng.reshape(1, d), w_in.astype(BF16), w_grp.astype(BF16),
      scale.reshape(1, width), w_out.astype(BF16))


def _rope_tables(seq):
    rows = seq // GRID_W
    row = jnp.repeat(jnp.arange(rows), GRID_W).astype(F32)
    col = jnp.tile(jnp.arange(GRID_W), rows).astype(F32)
    n_freq = HEAD_DIM // 4
    inv = ROPE_THETA ** (-jnp.arange(n_freq, dtype=F32) / n_freq)
    ang = jnp.concatenate([row[:, None] * inv, col[:, None] * inv], axis=-1)
    cos, sin = jnp.cos(ang), jnp.sin(ang)
    reps = LANES // HEAD_DIM
    cos_t = jnp.tile(jnp.concatenate([cos, cos], axis=-1), (1, reps))
    sin_t = jnp.tile(jnp.concatenate([-sin, sin], axis=-1), (1, reps))
    return cos_t, sin_t


def _rope_slab(t, cos, sin, first_half):
    half = HEAD_DIM // 2
    partner = jnp.where(first_half, pltpu.roll(t, LANES - half, axis=1), pltpu.roll(t, half, axis=1))
    return t * cos + partner * sin


def _qkvg_kernel(x_ref, mod_ref, ng_ref, win_ref, cos_ref, sin_ref, q_ref, k_ref, v_ref, sg_ref):
    d = x_ref.shape[2]
    qw, kvw = q_ref.shape[2], k_ref.shape[2]
    shift, scale, _ = _split_mod(mod_ref[0], d)
    hb = (_rms(x_ref[0], ng_ref[...]) * (1.0 + scale) + shift).astype(BF16)
    cos, sin = cos_ref[...], sin_ref[...]
    lane = lax.broadcasted_iota(jnp.int32, (1, LANES), 1)
    first_half = (lane % HEAD_DIM) < HEAD_DIM // 2
    sm_scale = HEAD_DIM ** -0.5

    q = _bdot(hb, win_ref[:, :qw])
    for j in range(qw // LANES):
        sl = slice(j * LANES, (j + 1) * LANES)
        q_ref[0, :, sl] = (_rope_slab(q[:, sl], cos, sin, first_half) * sm_scale).astype(BF16)
    k = _bdot(hb, win_ref[:, qw:qw + kvw])
    for j in range(kvw // LANES):
        sl = slice(j * LANES, (j + 1) * LANES)
        k_ref[0, :, sl] = _rope_slab(k[:, sl], cos, sin, first_half).astype(BF16)
    v_ref[0] = _bdot(hb, win_ref[:, qw + kvw:qw + 2 * kvw]).astype(BF16)
    sg_ref[0] = jax.nn.silu(_bdot(hb, win_ref[:, qw + 2 * kvw:])).astype(BF16)


def _qkvg_proj(x, mod, ng, w_in, cos_t, sin_t, *, tm):
    bn, seq, d = x.shape
    kvw = KV_HEADS * HEAD_DIM
    qw = Q_GROUP * kvw
    tok = lambda w: pl.BlockSpec((1, tm, w), lambda b, i: (b, i, 0))
    return pl.pallas_call(
        _qkvg_kernel,
        grid=(bn, seq // tm),
        in_specs=[
            tok(d),
            _mod_spec(mod),
            _const_spec((1, d)),
            _const_spec((d, 2 * qw + 2 * kvw)),
            pl.BlockSpec((tm, LANES), lambda b, i: (i, 0)),
            pl.BlockSpec((tm, LANES), lambda b, i: (i, 0)),
        ],
        out_specs=[tok(qw), tok(kvw), tok(kvw), tok(qw)],
        out_shape=[jax.ShapeDtypeStruct((bn, seq, qw), BF16),
                   jax.ShapeDtypeStruct((bn, seq, kvw), BF16),
                   jax.ShapeDtypeStruct((bn, seq, kvw), BF16),
                   jax.ShapeDtypeStruct((bn, seq, qw), BF16)],
        compiler_params=_params(2),
        name="attn_qkvg_proj",
    )(x, mod, ng.reshape(1, d), w_in, cos_t, sin_t)


def _ctx_kv_kernel(x_ref, mod_ref, ng_ref, wkv_ref, k_ref, v_ref):
    d = x_ref.shape[2]
    kvw = k_ref.shape[2]
    shift, scale, _ = _split_mod(mod_ref[0], d)
    hb = (_rms(x_ref[0], ng_ref[...]) * (1.0 + scale) + shift).astype(BF16)
    kv = _bdot(hb, wkv_ref[...])
    k_ref[0] = kv[:, :kvw].astype(BF16)
    v_ref[0] = kv[:, kvw:].astype(BF16)


def _ctx_kv_proj(ctx, mod, ng, w_kv):
    bn, lc, d = ctx.shape
    kvw = w_kv.shape[1] // 2
    tok = lambda w: pl.BlockSpec((1, lc, w), lambda b, i: (b, i, 0))
    return pl.pallas_call(
        _ctx_kv_kernel,
        grid=(bn, 1),
        in_specs=[tok(d), _mod_spec(mod), _const_spec((1, d)), _const_spec((d, 2 * kvw))],
        out_specs=[tok(kvw), tok(kvw)],
        out_shape=[jax.ShapeDtypeStruct((bn, lc, kvw), BF16)] * 2,
        compiler_params=_params(2),
        name="attn_ctx_kv_proj",
    )(ctx, mod, ng.reshape(1, d), w_kv)


def _attn_kernel(sink_ref, x_ref, mod_ref, q_ref, kp_ref, km_ref, kn_ref, vp_ref, vm_ref, vn_ref,
                 kc_ref, vc_ref, sg_ref, wout_ref, o_ref, *, seq):
    d = x_ref.shape[2]
    kvw = km_ref.shape[2]
    n = pl.program_id(1)
    _, _, gate = _split_mod(mod_ref[0], d)
    k_all = jnp.concatenate([kp_ref[0], km_ref[0], kn_ref[0], kc_ref[0]], axis=0)
    v_all = jnp.concatenate([vp_ref[0], vm_ref[0], vn_ref[0], vc_ref[0]], axis=0)
    span = ATT_BLOCK + 2 * ATT_WINDOW
    n_keys = k_all.shape[0]
    rows = Q_GROUP * ATT_BLOCK

    qi = lax.broadcasted_iota(jnp.int32, (rows, 1), 0) % ATT_BLOCK
    si = lax.broadcasted_iota(jnp.int32, (1, n_keys), 1)
    kpos = n * ATT_BLOCK - ATT_WINDOW + si
    band = jnp.abs(qi + ATT_WINDOW - si) <= ATT_WINDOW
    valid = (si >= span) | (band & (kpos >= 0) & (kpos < seq))
    neg = jnp.finfo(F32).min
    head_of_lane = lax.broadcasted_iota(jnp.int32, (1, kvw), 1) // HEAD_DIM

    o_slabs = [jnp.zeros((ATT_BLOCK, kvw), F32) for _ in range(Q_GROUP)]
    for kh in range(KV_HEADS):
        own = head_of_lane == kh
        lhs = jnp.concatenate(
            [jnp.where(own, q_ref[0, :, g * kvw:(g + 1) * kvw], jnp.zeros((), BF16))
             for g in range(Q_GROUP)], axis=0)
        s = lax.dot_general(lhs, k_all, (((1,), (1,)), ((), ())), preferred_element_type=F32)
        s = jnp.where(valid, s, neg)
        snk = jnp.concatenate(
            [jnp.full((ATT_BLOCK, 1), sink_ref[kh * Q_GROUP + g], F32) for g in range(Q_GROUP)], axis=0)
        m = jnp.maximum(jnp.max(s, axis=-1, keepdims=True), snk)
        e = jnp.exp(s - m)
        den = jnp.sum(e, axis=-1, keepdims=True) + jnp.exp(snk - m)
        pv = _bdot(e.astype(BF16), v_all) * (1.0 / den)
        for g in range(Q_GROUP):
            o_slabs[g] = jnp.where(own, pv[g * ATT_BLOCK:(g + 1) * ATT_BLOCK], o_slabs[g])

    o = jnp.concatenate(o_slabs, axis=1)
    y = _bdot((o * sg_ref[0].astype(F32)).astype(BF16), wout_ref[...])
    o_ref[0] = x_ref[0] + gate * y


def _attn_layer(x, mod, q, k, v, kc, vc, sg, sink, w_out):
    bn, seq, d = x.shape
    kvw = k.shape[2]
    qw = q.shape[2]
    lc = kc.shape[1]
    nb = seq // ATT_BLOCK
    tok = lambda w: pl.BlockSpec((1, ATT_BLOCK, w), lambda b, i: (b, i, 0))
    prev = pl.BlockSpec((1, ATT_BLOCK, kvw), lambda b, i: (b, jnp.maximum(i - 1, 0), 0))
    nxt = pl.BlockSpec((1, ATT_BLOCK, kvw), lambda b, i: (b, jnp.minimum(i + 1, nb - 1), 0))
    ctx = pl.BlockSpec((1, lc, kvw), lambda b, i: (b, 0, 0))
    return pl.pallas_call(
        functools.partial(_attn_kernel, seq=seq),
        grid=(bn, nb),
        in_specs=[
            pl.BlockSpec(memory_space=pltpu.SMEM),
            tok(d), _mod_spec(mod), tok(qw),
            prev, tok(kvw), nxt, prev, tok(kvw), nxt,
            ctx, ctx, tok(qw), _const_spec((qw, d)),
        ],
        out_specs=tok(d),
        out_shape=jax.ShapeDtypeStruct(x.shape, F32),
        compiler_params=_params(2),
        name="attn_layer",
    )(sink, x, mod, q, k, k, k, v, v, v, kc, vc, sg, w_out)


def _group_major_cols(w):
    lead = w.shape[:-1]
    w = w.reshape(lead + (KV_HEADS, Q_GROUP, HEAD_DIM))
    return jnp.swapaxes(w, -3, -2).reshape(lead + (KV_HEADS * Q_GROUP * HEAD_DIM,))


def kernel(x, c, ctx, c_ctx, ada_w, ada_b, norm_g, final_g, gmlp_w_in, gmlp_vnorm_g, gmlp_vnorm_b,
           gmlp_w_s, gmlp_b_s, gmlp_w_out, pool_w_in, pool_w_grp, pool_scale, pool_w_out,
           attn_w_in, attn_sink, attn_w_out):
    bn, seq, d = x.shape
    depth = ada_w.shape[0]
    lc = ctx.shape[1]
    n_mixers = 3
    kvw = KV_HEADS * HEAD_DIM
    qw = Q_GROUP * kvw

    pad = (-(bn + 1)) % COND_ROWS_ALIGN
    c_all = jnp.concatenate([c, c_ctx[None], jnp.zeros((pad, d), c.dtype)], axis=0)
    mod = _modulation(c_all, ada_w, ada_b)
    cos_t, sin_t = _rope_tables(seq)

    for i in range(depth):
        kind, j = i % n_mixers, i // n_mixers
        ctx_out = any(l % n_mixers == 2 for l in range(i + 1, depth))
        last = i == depth - 1
        mod_lat = mod[i, :bn].reshape(bn, 1, 3 * d)
        mod_ctx = mod[i, bn:bn + 1].reshape(1, 1, 3 * d)
        if kind == 0:
            run = functools.partial(
                _gmlp_layer, ng=norm_g[i], w_in=gmlp_w_in[j], vn_g=gmlp_vnorm_g[j],
                vn_b=gmlp_vnorm_b[j], w_s=gmlp_w_s[j], b_s=gmlp_b_s[j], w_out=gmlp_w_out[j],
                final_g=final_g, tm=2 * A_CHUNK)
            x = run(x, mod_lat, final_norm=last)
            if ctx_out:
                ctx = run(ctx, mod_ctx, final_norm=False)
        elif kind == 1:
            run = functools.partial(
                _pool_layer, ng=norm_g[i], w_in=pool_w_in[j], w_grp=pool_w_grp[j],
                scale=pool_scale[j], w_out=pool_w_out[j], tm=256)
            x = run(x, mod_lat)
            if ctx_out:
                ctx = run(ctx, mod_ctx)
        else:
            assert not ctx_out, "context-stream attention output is not needed by this depth"
            w_in = attn_w_in[j]
            w_q, w_kv, w_g = w_in[:, :qw], w_in[:, qw:qw + 2 * kvw], w_in[:, qw + 2 * kvw:]
            w_lat = jnp.concatenate([_group_major_cols(w_q), w_kv, _group_major_cols(w_g)],
                                    axis=1).astype(BF16)
            w_out = _group_major_cols(attn_w_out[j].T).T.astype(BF16)
            q, k, v, sg = _qkvg_proj(x, mod_lat, norm_g[i], w_lat, cos_t, sin_t, tm=256)
            kc, vc = _ctx_kv_proj(ctx, mod_ctx, norm_g[i], w_kv.astype(BF16))
            x = _attn_layer(x, mod_lat, q, k, v, kc, vc, sg, attn_sink[j], w_out)
    if depth % n_mixers != 1:
        raise NotImplementedError("final RMSNorm is fused into a trailing gMLP layer")
    return x
```

```python
import functools

import jax
import jax.numpy as jnp
from jax import lax
from jax.experimental import pallas as pl
from jax.experimental.pallas import tpu as pltpu

NORM_EPS = 1e-6
GRID_W = 64
A_CHUNK = 128
B_WINDOWS = (2, 4, 8, 16)
B_HALO = 8
HEAD_DIM = 64
KV_HEADS = 4
Q_GROUP = 4
ATT_WINDOW = 128
ATT_BLOCK = 128
ROPE_THETA = 10000.0
LOG2E = 1.4426950408889634
SM_ROWS = 32
GMLP_LOOKAHEAD = 2
GMLP_SUB_ROWS = 512
ATT_LOOKAHEAD = 1
POOL_LOOKAHEAD = 2
GMLP_OUT_SPLIT = 4
LANES = 128
COND_ROWS_ALIGN = 8

F32 = jnp.float32
BF16 = jnp.bfloat16

VMEM_LIMIT_BYTES = 56 * 1024 * 1024
TOKEN_TILE = 512
QKV_TILE = 1024


def _const_spec(shape):
    zeros = (0,) * len(shape)
    return pl.BlockSpec(shape, lambda *_: zeros, pipeline_mode=pl.Buffered(1))


def _params(n_axes):
    return pltpu.CompilerParams(
        dimension_semantics=("parallel",) * n_axes,
        vmem_limit_bytes=VMEM_LIMIT_BYTES,
    )


def _rms(x, g):
    return x * lax.rsqrt(jnp.mean(x * x, axis=-1, keepdims=True) + NORM_EPS) * g


def _mod_norm(x, g, scale, shift):
    gain = g * (1.0 + scale)
    return x * lax.rsqrt(jnp.mean(x * x, axis=-1, keepdims=True) + NORM_EPS) * gain + shift


def _split_mod(mod, d):
    return mod[:, :d], mod[:, d:2 * d], mod[:, 2 * d:]


GELU_A = -2.0 * 0.7978845608028654 * LOG2E
GELU_B = GELU_A * 0.044715


def _gelu_exp(x):
    return jnp.exp2(x * (GELU_A + GELU_B * (x * x)))


def _gelu(x):
    return x / (1.0 + _gelu_exp(x))


def _gelu_times_silu(u, t):
    return (u * t) / ((1.0 + _gelu_exp(u)) * (1.0 + jnp.exp2(t * -LOG2E)))


def _bdot(a, b):
    return jnp.dot(a, b, preferred_element_type=F32)


def _mod_kernel(c_ref, w_ref, b_ref, o_ref):
    cond = jax.nn.silu(c_ref[...])
    o_ref[0] = jnp.dot(cond, w_ref[0], preferred_element_type=F32,
                       precision=lax.Precision.HIGHEST) + b_ref[0]


def _modulation(c_all, ada_w, ada_b):
    depth, d, d3 = ada_w.shape
    rows = c_all.shape[0]
    tn = d
    return pl.pallas_call(
        _mod_kernel,
        grid=(depth, d3 // tn),
        in_specs=[
            pl.BlockSpec((rows, d), lambda i, j: (0, 0)),
            pl.BlockSpec((1, d, tn), lambda i, j: (i, 0, j)),
            pl.BlockSpec((1, 1, tn), lambda i, j: (i, 0, j)),
        ],
        out_specs=pl.BlockSpec((1, rows, tn), lambda i, j: (i, 0, j)),
        out_shape=jax.ShapeDtypeStruct((depth, rows, d3), F32),
        compiler_params=_params(2),
        name="adaln_modulation",
    )(c_all, ada_w, ada_b.reshape(depth, 1, d3))


def _mod_spec(mod):
    if mod.shape[0] == 1:
        return pl.BlockSpec((1, 1, mod.shape[2]), lambda b, i: (0, 0, 0))
    return pl.BlockSpec((1, 1, mod.shape[2]), lambda b, i: (b, 0, 0))


def _gmlp_kernel(x_ref, mod_ref, ng_ref, win_ref, vg_ref, vb_ref, ws_ref, bs_ref, wout_ref, fg_ref,
                 o_ref, hb_scr, v_scr, ug_scr, y_scr, *, final_norm):
    tm, d = x_ref.shape[1], x_ref.shape[2]
    width = wout_ref.shape[0]
    groups = ws_ref.shape[0]
    gw = width // groups
    shift, scale, gate = _split_mod(mod_ref[0], d)
    n_sub = tm // GMLP_SUB_ROWS
    sub_rows = [slice(h * GMLP_SUB_ROWS, (h + 1) * GMLP_SUB_ROWS) for h in range(n_sub)]

    def lane_tile_sum(t):
        return functools.reduce(jnp.add, [t[:, c:c + LANES] for c in range(0, gw, LANES)])

    def project_v(rows):
        hb = hb_scr[rows, :]
        row_sum = None
        for g in range(groups):
            lo = g * gw
            vg = _gelu(_bdot(hb, win_ref[:, width + lo:width + lo + gw]))
            v_scr[rows, lo:lo + gw] = vg
            row_sum = lane_tile_sum(vg) if row_sum is None else row_sum + lane_tile_sum(vg)
        return jnp.sum(row_sum, axis=-1, keepdims=True) * (1.0 / width)

    def inv_std(rows, mu):
        sq_sum = None
        for g in range(groups):
            cen = v_scr[rows, g * gw:(g + 1) * gw] - mu
            sq_sum = lane_tile_sum(cen * cen) if sq_sum is None else sq_sum + lane_tile_sum(cen * cen)
        return lax.rsqrt(jnp.sum(sq_sum, axis=-1, keepdims=True) * (1.0 / width) + NORM_EPS)

    def gated_u(rows, g):
        lo = g * gw
        hb = hb_scr[rows, :]
        u = _bdot(hb, win_ref[:, lo:lo + gw])
        gt = _bdot(hb, win_ref[:, 2 * width + lo:2 * width + lo + gw])
        ug_scr[rows, lo:lo + gw] = _gelu_times_silu(u, gt)

    def mix_positions(rows, g, mu, rstd):
        lo = g * gw
        vn = ((v_scr[rows, lo:lo + gw] - mu) * rstd * vg_ref[:, lo:lo + gw]
              + vb_ref[:, lo:lo + gw]).astype(BF16)
        bias = bs_ref[:, g:g + 1]
        s = jnp.concatenate(
            [_bdot(ws_ref[g], vn[c * A_CHUNK:(c + 1) * A_CHUNK]) + bias
             for c in range(GMLP_SUB_ROWS // A_CHUNK)], axis=0)
        y_scr[rows, lo:lo + gw] = (ug_scr[rows, lo:lo + gw] * s).astype(BF16)

    def groups_of(rows, mu, rstd):
        acc = None
        for g in range(GMLP_LOOKAHEAD):
            gated_u(rows, g)
        for g in range(groups):
            if g + GMLP_LOOKAHEAD < groups:
                gated_u(rows, g + GMLP_LOOKAHEAD)
            mix_positions(rows, g, mu, rstd)
            if (g + 1) % GMLP_OUT_SPLIT == 0:
                k = slice((g + 1 - GMLP_OUT_SPLIT) * gw, (g + 1) * gw)
                part = _bdot(y_scr[rows, k], wout_ref[k, :])
                acc = part if acc is None else acc + part
        return acc

    def finish(rows, acc):
        out = x_ref[0, rows, :] + gate * acc
        if final_norm:
            out = _rms(out, fg_ref[...])
        o_ref[0, rows, :] = out

    for rows in sub_rows:
        hb_scr[rows, :] = _mod_norm(x_ref[0, rows, :], ng_ref[...], scale, shift).astype(BF16)
    mu = project_v(sub_rows[0])
    for h, rows in enumerate(sub_rows):
        mu_next = project_v(sub_rows[h + 1]) if h + 1 < n_sub else None
        rstd = inv_std(rows, mu)
        finish(rows, groups_of(rows, mu, rstd))
        mu = mu_next


def _gmlp_layer(x, mod, ng, w_in, vn_g, vn_b, w_s, b_s, w_out, final_g, *, final_norm, tm):
    bn, seq, d = x.shape
    width = w_out.shape[0]
    groups = w_s.shape[0]
    return pl.pallas_call(
        functools.partial(_gmlp_kernel, final_norm=final_norm),
        grid=(bn, seq // tm),
        in_specs=[
            pl.BlockSpec((1, tm, d), lambda b, i: (b, i, 0)),
            _mod_spec(mod),
            _const_spec((1, d)),
            _const_spec((d, 3 * width)),
            _const_spec((1, width)),
            _const_spec((1, width)),
            _const_spec((groups, A_CHUNK, A_CHUNK)),
            _const_spec((A_CHUNK, groups)),
            _const_spec((width, d)),
            _const_spec((1, d)),
        ],
        out_specs=pl.BlockSpec((1, tm, d), lambda b, i: (b, i, 0)),
        out_shape=jax.ShapeDtypeStruct(x.shape, F32),
        scratch_shapes=[pltpu.VMEM((tm, d), BF16), pltpu.VMEM((tm, width), F32),
                        pltpu.VMEM((tm, width), F32), pltpu.VMEM((tm, width), BF16)],
        compiler_params=_params(2),
        name="gmlp_layer",
    )(x, mod, ng.reshape(1, d), w_in.astype(BF16), vn_g.reshape(1, width), vn_b.reshape(1, width),
      w_s.astype(BF16), b_s.T, w_out.astype(BF16), final_g.reshape(1, d))


def _pool_kernel(xm_ref, xp_ref, xn_ref, mod_ref, ng_ref, win_ref, wgrp_ref, sc_ref, wout_ref,
                 o_ref, p_scr, y_scr, *, seq):
    tm, d = xm_ref.shape[1], xm_ref.shape[2]
    width = wout_ref.shape[0]
    gw = width // len(B_WINDOWS)
    i = pl.program_id(1)
    x = xm_ref[0]
    shift, scale, gate = _split_mod(mod_ref[0], d)
    xe = jnp.concatenate([xp_ref[0], x, xn_ref[0]], axis=0)
    he = _mod_norm(xe, ng_ref[...], scale, shift)

    pos_e = i * tm - B_HALO + lax.broadcasted_iota(jnp.int32, (tm + 2 * B_HALO, 1), 0)
    in_seq = (pos_e >= 0) & (pos_e < seq)
    hb_ext = he.astype(BF16)
    hb = he[B_HALO:B_HALO + tm].astype(BF16)

    def project_p(g):
        cols = slice(g * gw, (g + 1) * gw)
        p_scr[:, cols] = jnp.where(in_seq, _bdot(hb_ext, win_ref[:, cols]), 0.0)

    for g in range(POOL_LOOKAHEAD):
        project_p(g)
    pos = i * tm + lax.broadcasted_iota(jnp.int32, (tm, 1), 0)
    n_ext = tm + 2 * B_HALO
    for g, w in enumerate(B_WINDOWS):
        lo = g * gw
        gt = _bdot(hb, win_ref[:, width + lo:width + lo + gw])
        if g + POOL_LOOKAHEAD < len(B_WINDOWS):
            project_p(g + POOL_LOOKAHEAD)
        pe = p_scr[:, lo:lo + gw]
        acc = pe + pltpu.roll(pe, 1, axis=0)
        half = 1
        while 2 * half < w:
            acc = pltpu.roll(acc, half, axis=0) + pltpu.roll(acc, n_ext - half, axis=0)
            half *= 2
        acc = acc[B_HALO:B_HALO + tm]
        cnt = jnp.minimum(pos + w // 2, seq) - jnp.maximum(pos - w // 2, 0)
        pooled = acc * (1.0 / cnt.astype(F32)) - p_scr[B_HALO:B_HALO + tm, lo:lo + gw]
        mixed = _bdot(pooled.astype(BF16), wgrp_ref[g]) * sc_ref[:, lo:lo + gw]
        y_scr[:, lo:lo + gw] = (mixed * jax.nn.silu(gt)).astype(BF16)

    o_ref[0] = x + gate * _bdot(y_scr[...], wout_ref[...])


def _pool_layer(x, mod, ng, w_in, w_grp, scale, w_out, *, tm):
    bn, seq, d = x.shape
    width = w_out.shape[0]
    n_grp, gw, _ = w_grp.shape
    per_tile = tm // B_HALO
    n_halo_blocks = seq // B_HALO
    return pl.pallas_call(
        functools.partial(_pool_kernel, seq=seq),
        grid=(bn, seq // tm),
        in_specs=[
            pl.BlockSpec((1, tm, d), lambda b, i: (b, i, 0)),
            pl.BlockSpec((1, B_HALO, d), lambda b, i: (b, jnp.maximum(i * per_tile - 1, 0), 0)),
            pl.BlockSpec((1, B_HALO, d),
                         lambda b, i: (b, jnp.minimum((i + 1) * per_tile, n_halo_blocks - 1), 0)),
            _mod_spec(mod),
            _const_spec((1, d)),
            _const_spec((d, 2 * width)),
            _const_spec((n_grp, gw, gw)),
            _const_spec((1, width)),
            _const_spec((width, d)),
        ],
        out_specs=pl.BlockSpec((1, tm, d), lambda b, i: (b, i, 0)),
        out_shape=jax.ShapeDtypeStruct(x.shape, F32),
        scratch_shapes=[pltpu.VMEM((tm + 2 * B_HALO, width), F32), pltpu.VMEM((tm, width), BF16)],
        compiler_params=_params(2),
        name="pool_layer",
    )(x, x, x, mod, ng.reshape(1, d), w_in.astype(BF16), w_grp.astype(BF16),
      scale.reshape(1, width), w_out.astype(BF16))


def _rope_tables(seq):
    rows = seq // GRID_W
    row = jnp.repeat(jnp.arange(rows), GRID_W).astype(F32)
    col = jnp.tile(jnp.arange(GRID_W), rows).astype(F32)
    n_freq = HEAD_DIM // 4
    inv = ROPE_THETA ** (-jnp.arange(n_freq, dtype=F32) / n_freq)
    ang = jnp.concatenate([row[:, None] * inv, col[:, None] * inv], axis=-1)
    cos, sin = jnp.cos(ang), jnp.sin(ang)
    reps = LANES // HEAD_DIM
    cos_t = jnp.tile(jnp.concatenate([cos, cos], axis=-1), (1, reps))
    sin_t = jnp.tile(jnp.concatenate([-sin, sin], axis=-1), (1, reps))
    q_scale = LOG2E * HEAD_DIM ** -0.5
    return cos_t * q_scale, sin_t * q_scale, cos_t, sin_t


def _rope_slab(t, cos, sin, first_half):
    half = HEAD_DIM // 2
    partner = jnp.where(first_half, pltpu.roll(t, LANES - half, axis=1), pltpu.roll(t, half, axis=1))
    return t * cos + partner * sin


def _qkvg_kernel(x_ref, mod_ref, ng_ref, win_ref, cosq_ref, sinq_ref, cos_ref, sin_ref,
                 q_ref, kt_ref, v_ref, sg_ref):
    d = x_ref.shape[2]
    qw, kvw = q_ref.shape[2], kt_ref.shape[1]
    shift, scale, _ = _split_mod(mod_ref[0], d)
    hb = _mod_norm(x_ref[0], ng_ref[...], scale, shift).astype(BF16)
    lane = lax.broadcasted_iota(jnp.int32, (1, LANES), 1)
    first_half = (lane % HEAD_DIM) < HEAD_DIM // 2

    q = _bdot(hb, win_ref[:, :qw])
    for j in range(qw // LANES):
        sl = slice(j * LANES, (j + 1) * LANES)
        q_ref[0, :, sl] = _rope_slab(q[:, sl], cosq_ref[...], sinq_ref[...], first_half).astype(BF16)
    k = _bdot(hb, win_ref[:, qw:qw + kvw])
    for j in range(kvw // LANES):
        sl = slice(j * LANES, (j + 1) * LANES)
        kt_ref[0, sl, :] = _rope_slab(k[:, sl], cos_ref[...], sin_ref[...], first_half).T.astype(BF16)
    v_ref[0] = _bdot(hb, win_ref[:, qw + kvw:qw + 2 * kvw]).astype(BF16)
    sg_ref[0] = jax.nn.silu(_bdot(hb, win_ref[:, qw + 2 * kvw:])).astype(BF16)


def _qkvg_proj(x, mod, ng, w_in, rope, *, tm):
    bn, seq, d = x.shape
    kvw = KV_HEADS * HEAD_DIM
    qw = Q_GROUP * kvw
    tok = lambda w: pl.BlockSpec((1, tm, w), lambda b, i: (b, i, 0))
    table = pl.BlockSpec((tm, LANES), lambda b, i: (i, 0))
    return pl.pallas_call(
        _qkvg_kernel,
        grid=(bn, seq // tm),
        in_specs=[
            tok(d),
            _mod_spec(mod),
            _const_spec((1, d)),
            _const_spec((d, 2 * qw + 2 * kvw)),
            table, table, table, table,
        ],
        out_specs=[tok(qw), pl.BlockSpec((1, kvw, tm), lambda b, i: (b, 0, i)), tok(kvw), tok(qw)],
        out_shape=[jax.ShapeDtypeStruct((bn, seq, qw), BF16),
                   jax.ShapeDtypeStruct((bn, kvw, seq), BF16),
                   jax.ShapeDtypeStruct((bn, seq, kvw), BF16),
                   jax.ShapeDtypeStruct((bn, seq, qw), BF16)],
        compiler_params=_params(2),
        name="attn_qkvg_proj",
    )(x, mod, ng.reshape(1, d), w_in, *rope)


def _ctx_kv_kernel(x_ref, mod_ref, ng_ref, wkv_ref, kt_ref, v_ref):
    d = x_ref.shape[2]
    kvw = v_ref.shape[2]
    shift, scale, _ = _split_mod(mod_ref[0], d)
    hb = _mod_norm(x_ref[0], ng_ref[...], scale, shift).astype(BF16)
    kv = _bdot(hb, wkv_ref[...])
    kt_ref[0] = kv[:, :kvw].T.astype(BF16)
    v_ref[0] = kv[:, kvw:].astype(BF16)


def _ctx_kv_proj(ctx, mod, ng, w_kv):
    bn, lc, d = ctx.shape
    kvw = w_kv.shape[1] // 2
    tok = lambda w: pl.BlockSpec((1, lc, w), lambda b, i: (b, i, 0))
    return pl.pallas_call(
        _ctx_kv_kernel,
        grid=(bn, 1),
        in_specs=[tok(d), _mod_spec(mod), _const_spec((1, d)), _const_spec((d, 2 * kvw))],
        out_specs=[pl.BlockSpec((1, kvw, lc), lambda b, i: (b, 0, i)), tok(kvw)],
        out_shape=[jax.ShapeDtypeStruct((bn, kvw, lc), BF16), jax.ShapeDtypeStruct((bn, lc, kvw), BF16)],
        compiler_params=_params(2),
        name="attn_ctx_kv_proj",
    )(ctx, mod, ng.reshape(1, d), w_kv)


def _attn_kernel(sink_ref, x_ref, mod_ref, q_ref, kp_ref, km_ref, kn_ref, vp_ref, vm_ref, vn_ref,
                 kc_ref, vc_ref, sg_ref, wout_ref, o_ref,
                 k_scr, v_scr, vc_scr, s_scr, p_scr, r_scr, og_scr, *, seq):
    tq, d = x_ref.shape[1], x_ref.shape[2]
    kvw = vm_ref.shape[2]
    lc = vc_ref.shape[1]
    span = ATT_BLOCK + 2 * ATT_WINDOW
    rows = Q_GROUP * ATT_BLOCK
    n_blk = tq // ATT_BLOCK
    i = pl.program_id(1)
    neg = jnp.finfo(F32).min

    k_scr[:, 0:ATT_WINDOW] = kp_ref[0]
    k_scr[:, ATT_WINDOW:ATT_WINDOW + tq] = km_ref[0]
    k_scr[:, ATT_WINDOW + tq:] = kn_ref[0]
    ones = jnp.ones((tq + 2 * ATT_WINDOW, LANES), BF16)
    for t in range(kvw // LANES):
        src, dst = slice(t * LANES, (t + 1) * LANES), slice(2 * t * LANES, (2 * t + 1) * LANES)
        v_scr[0:ATT_WINDOW, dst] = vp_ref[0, :, src]
        v_scr[ATT_WINDOW:ATT_WINDOW + tq, dst] = vm_ref[0, :, src]
        v_scr[ATT_WINDOW + tq:, dst] = vn_ref[0, :, src]
        vc_scr[:, dst] = vc_ref[0, :, src]
        v_scr[:, (2 * t + 1) * LANES:(2 * t + 2) * LANES] = ones
        vc_scr[:, (2 * t + 1) * LANES:(2 * t + 2) * LANES] = ones[:lc]

    head_of_lane = lax.broadcasted_iota(jnp.int32, (1, kvw), 1) // HEAD_DIM
    lane = lax.broadcasted_iota(jnp.int32, (1, LANES), 1)
    low_head = lane < HEAD_DIM

    def softmax_rows(kh, r, sink, off_prev, off_next):
        sl = slice(r * SM_ROWS, (r + 1) * SM_ROWS)
        qi = (r * SM_ROWS) % ATT_BLOCK + lax.broadcasted_iota(jnp.int32, (SM_ROWS, 1), 0)
        parts = [s_scr[kh, sl, c * LANES:(c + 1) * LANES] for c in range((span + lc) // LANES)]
        parts[0] = jnp.where(lane >= qi + off_prev, parts[0], neg)
        last = span // LANES - 1
        parts[last] = jnp.where(lane <= qi - off_next, parts[last], neg)
        m = functools.reduce(jnp.maximum, parts)
        m = jnp.maximum(jnp.max(m, axis=-1, keepdims=True), sink)
        for c, part in enumerate(parts):
            p_scr[kh, sl, c * LANES:(c + 1) * LANES] = jnp.exp2(part - m).astype(BF16)
        r_scr[kh, sl, :] = jnp.broadcast_to(jnp.exp2(sink - m), (SM_ROWS, LANES))

    heads_per_tile = LANES // HEAD_DIM
    off_first = jnp.where(i > 0, 0, ATT_BLOCK)
    off_last = jnp.where(i < seq // tq - 1, 0, ATT_BLOCK)

    def scores(j, kh):
        own = head_of_lane == kh
        q_rows = slice(j * ATT_BLOCK, (j + 1) * ATT_BLOCK)
        lhs = jnp.concatenate(
            [jnp.where(own, q_ref[0, q_rows, g * kvw:(g + 1) * kvw], jnp.zeros((), BF16))
             for g in range(Q_GROUP)], axis=0)
        s_scr[kh, :, :span] = _bdot(lhs, k_scr[:, j * ATT_BLOCK:j * ATT_BLOCK + span])
        s_scr[kh, :, span:] = _bdot(lhs, kc_ref[0])

    items = [(j, kh) for j in range(n_blk) for kh in range(KV_HEADS)]
    for item in items[:ATT_LOOKAHEAD]:
        scores(*item)
    pv_pair = []
    for idx, (j, kh) in enumerate(items):
        t = kh // heads_per_tile
        if idx + ATT_LOOKAHEAD < len(items):
            scores(*items[idx + ATT_LOOKAHEAD])
        off_prev = off_first if j == 0 else 0
        off_next = off_last if j == n_blk - 1 else 0
        for r in range(rows // SM_ROWS):
            g = r * SM_ROWS // ATT_BLOCK
            softmax_rows(kh, r, sink_ref[kh * Q_GROUP + g] * LOG2E, off_prev, off_next)
        v_cols = slice(2 * t * LANES, (2 * t + 2) * LANES)
        pv = (_bdot(p_scr[kh, :, :span], v_scr[j * ATT_BLOCK:j * ATT_BLOCK + span, v_cols])
              + _bdot(p_scr[kh, :, span:], vc_scr[:, v_cols]))
        den = pv[:, LANES:] + r_scr[kh]
        pv_pair.append(pv[:, :LANES] * (1.0 / den))
        if len(pv_pair) == heads_per_tile:
            o_t = jnp.where(low_head, pv_pair[0], pv_pair[1])
            pv_pair = []
            q_rows = slice(j * ATT_BLOCK, (j + 1) * ATT_BLOCK)
            for g in range(Q_GROUP):
                cols = slice(g * kvw + t * LANES, g * kvw + (t + 1) * LANES)
                sg = sg_ref[0, q_rows, cols].astype(F32)
                og_scr[q_rows, cols] = (o_t[g * ATT_BLOCK:(g + 1) * ATT_BLOCK] * sg).astype(BF16)

    _, _, gate = _split_mod(mod_ref[0], d)
    o_ref[0] = x_ref[0] + gate * _bdot(og_scr[...], wout_ref[...])


def _attn_layer(x, mod, q, kt, v, kct, vc, sg, sink, w_out, *, tq):
    bn, seq, d = x.shape
    kvw = v.shape[2]
    qw = q.shape[2]
    lc = vc.shape[1]
    nb = seq // ATT_BLOCK
    per_tile = tq // ATT_BLOCK
    span = ATT_BLOCK + 2 * ATT_WINDOW
    rows = Q_GROUP * ATT_BLOCK
    prev_blk = lambda i: jnp.maximum(i * per_tile - 1, 0)
    next_blk = lambda i: jnp.minimum((i + 1) * per_tile, nb - 1)
    tok = lambda w: pl.BlockSpec((1, tq, w), lambda b, i: (b, i, 0))
    prev = pl.BlockSpec((1, ATT_BLOCK, kvw), lambda b, i: (b, prev_blk(i), 0))
    nxt = pl.BlockSpec((1, ATT_BLOCK, kvw), lambda b, i: (b, next_blk(i), 0))
    prev_t = pl.BlockSpec((1, kvw, ATT_BLOCK), lambda b, i: (b, 0, prev_blk(i)))
    main_t = pl.BlockSpec((1, kvw, tq), lambda b, i: (b, 0, i))
    next_t = pl.BlockSpec((1, kvw, ATT_BLOCK), lambda b, i: (b, 0, next_blk(i)))
    return pl.pallas_call(
        functools.partial(_attn_kernel, seq=seq),
        grid=(bn, seq // tq),
        in_specs=[
            pl.BlockSpec(memory_space=pltpu.SMEM),
            tok(d), _mod_spec(mod), tok(qw),
            prev_t, main_t, next_t, prev, tok(kvw), nxt,
            pl.BlockSpec((1, kvw, lc), lambda b, i: (b, 0, 0)),
            pl.BlockSpec((1, lc, kvw), lambda b, i: (b, 0, 0)),
            tok(qw), _const_spec((qw, d)),
        ],
        out_specs=tok(d),
        out_shape=jax.ShapeDtypeStruct(x.shape, F32),
        scratch_shapes=[
            pltpu.VMEM((kvw, tq + 2 * ATT_WINDOW), BF16),
            pltpu.VMEM((tq + 2 * ATT_WINDOW, 2 * kvw), BF16),
            pltpu.VMEM((lc, 2 * kvw), BF16),
            pltpu.VMEM((KV_HEADS, rows, span + lc), F32),
            pltpu.VMEM((KV_HEADS, rows, span + lc), BF16),
            pltpu.VMEM((KV_HEADS, rows, LANES), F32),
            pltpu.VMEM((tq, qw), BF16),
        ],
        compiler_params=_params(2),
        name="attn_layer",
    )(sink, x, mod, q, kt, kt, kt, v, v, v, kct, vc, sg, w_out)


def _group_major_cols(w):
    lead = w.shape[:-1]
    w = w.reshape(lead + (KV_HEADS, Q_GROUP, HEAD_DIM))
    return jnp.swapaxes(w, -3, -2).reshape(lead + (KV_HEADS * Q_GROUP * HEAD_DIM,))


def kernel(x, c, ctx, c_ctx, ada_w, ada_b, norm_g, final_g, gmlp_w_in, gmlp_vnorm_g, gmlp_vnorm_b,
           gmlp_w_s, gmlp_b_s, gmlp_w_out, pool_w_in, pool_w_grp, pool_scale, pool_w_out,
           attn_w_in, attn_sink, attn_w_out):
    bn, seq, d = x.shape
    depth = ada_w.shape[0]
    lc = ctx.shape[1]
    n_mixers = 3
    kvw = KV_HEADS * HEAD_DIM
    qw = Q_GROUP * kvw

    pad = (-(bn + 1)) % COND_ROWS_ALIGN
    c_all = jnp.concatenate([c, c_ctx[None], jnp.zeros((pad, d), c.dtype)], axis=0)
    mod = _modulation(c_all, ada_w, ada_b)
    rope = _rope_tables(seq)

    for i in range(depth):
        kind, j = i % n_mixers, i // n_mixers
        ctx_out = any(l % n_mixers == 2 for l in range(i + 1, depth))
        last = i == depth - 1
        mod_lat = mod[i, :bn].reshape(bn, 1, 3 * d)
        mod_ctx = mod[i, bn:bn + 1].reshape(1, 1, 3 * d)
        if kind == 0:
            run = functools.partial(
                _gmlp_layer, ng=norm_g[i], w_in=gmlp_w_in[j], vn_g=gmlp_vnorm_g[j],
                vn_b=gmlp_vnorm_b[j], w_s=gmlp_w_s[j], b_s=gmlp_b_s[j], w_out=gmlp_w_out[j],
                final_g=final_g, tm=TOKEN_TILE)
            x = run(x, mod_lat, final_norm=last)
            if ctx_out:
                ctx = run(ctx.reshape(1, bn * lc, d), mod_ctx, final_norm=False).reshape(bn, lc, d)
        elif kind == 1:
            run = functools.partial(
                _pool_layer, ng=norm_g[i], w_in=pool_w_in[j], w_grp=pool_w_grp[j],
                scale=pool_scale[j], w_out=pool_w_out[j])
            x = run(x, mod_lat, tm=TOKEN_TILE)
            if ctx_out:
                ctx = run(ctx, mod_ctx, tm=min(TOKEN_TILE, lc))
        else:
            assert not ctx_out, "context-stream attention output is not needed by this depth"
            w_in = attn_w_in[j]
            w_q, w_kv, w_g = w_in[:, :qw], w_in[:, qw:qw + 2 * kvw], w_in[:, qw + 2 * kvw:]
            w_lat = jnp.concatenate([_group_major_cols(w_q), w_kv, _group_major_cols(w_g)],
                                    axis=1).astype(BF16)
            w_out = _group_major_cols(attn_w_out[j].T).T.astype(BF16)
            q, k, v, sg = _qkvg_proj(x, mod_lat, norm_g[i], w_lat, rope, tm=QKV_TILE)
            kc, vc = _ctx_kv_proj(ctx, mod_ctx, norm_g[i], w_kv.astype(BF16))
            x = _attn_layer(x, mod_lat, q, k, v, kc, vc, sg, attn_sink[j], w_out, tq=TOKEN_TILE)
    if depth % n_mixers != 1:
        raise NotImplementedError("final RMSNorm is fused into a trailing gMLP layer")
    return x
```

```python
import functools

import jax
import jax.numpy as jnp
from jax import lax
from jax.experimental import pallas as pl
from jax.experimental.pallas import tpu as pltpu

NORM_EPS = 1e-6
GRID_W = 64
A_CHUNK = 128
B_WINDOWS = (2, 4, 8, 16)
B_HALO = 8
HEAD_DIM = 64
KV_HEADS = 4
Q_GROUP = 4
ATT_WINDOW = 128
ATT_BLOCK = 128
ROPE_THETA = 10000.0
LOG2E = 1.4426950408889634
SM_ROWS = 32
GMLP_LOOKAHEAD = 2
ATT_LOOKAHEAD = 1
LANES = 128
COND_ROWS_ALIGN = 8

F32 = jnp.float32
BF16 = jnp.bfloat16

VMEM_LIMIT_BYTES = 56 * 1024 * 1024
TOKEN_TILE = 512
QKV_TILE = 1024


def _const_spec(shape):
    zeros = (0,) * len(shape)
    return pl.BlockSpec(shape, lambda *_: zeros, pipeline_mode=pl.Buffered(1))


def _params(n_axes):
    return pltpu.CompilerParams(
        dimension_semantics=("parallel",) * n_axes,
        vmem_limit_bytes=VMEM_LIMIT_BYTES,
    )


def _rms(x, g):
    return x * lax.rsqrt(jnp.mean(x * x, axis=-1, keepdims=True) + NORM_EPS) * g


def _mod_norm(x, g, scale, shift):
    gain = g * (1.0 + scale)
    return x * lax.rsqrt(jnp.mean(x * x, axis=-1, keepdims=True) + NORM_EPS) * gain + shift


def _split_mod(mod, d):
    return mod[:, :d], mod[:, d:2 * d], mod[:, 2 * d:]


GELU_A = -2.0 * 0.7978845608028654 * LOG2E
GELU_B = GELU_A * 0.044715


def _gelu_exp(x):
    return jnp.exp2(x * (GELU_A + GELU_B * (x * x)))


def _gelu(x):
    return x / (1.0 + _gelu_exp(x))


def _gelu_times_silu(u, t):
    return (u * t) / ((1.0 + _gelu_exp(u)) * (1.0 + jnp.exp2(t * -LOG2E)))


def _bdot(a, b):
    return jnp.dot(a, b, preferred_element_type=F32)


def _mod_kernel(c_ref, w_ref, b_ref, o_ref):
    cond = jax.nn.silu(c_ref[...])
    o_ref[0] = jnp.dot(cond, w_ref[0], preferred_element_type=F32,
                       precision=lax.Precision.HIGHEST) + b_ref[0]


def _modulation(c_all, ada_w, ada_b):
    depth, d, d3 = ada_w.shape
    rows = c_all.shape[0]
    tn = d
    return pl.pallas_call(
        _mod_kernel,
        grid=(depth, d3 // tn),
        in_specs=[
            pl.BlockSpec((rows, d), lambda i, j: (0, 0)),
            pl.BlockSpec((1, d, tn), lambda i, j: (i, 0, j)),
            pl.BlockSpec((1, 1, tn), lambda i, j: (i, 0, j)),
        ],
        out_specs=pl.BlockSpec((1, rows, tn), lambda i, j: (i, 0, j)),
        out_shape=jax.ShapeDtypeStruct((depth, rows, d3), F32),
        compiler_params=_params(2),
        name="adaln_modulation",
    )(c_all, ada_w, ada_b.reshape(depth, 1, d3))


def _mod_spec(mod):
    if mod.shape[0] == 1:
        return pl.BlockSpec((1, 1, mod.shape[2]), lambda b, i: (0, 0, 0))
    return pl.BlockSpec((1, 1, mod.shape[2]), lambda b, i: (b, 0, 0))


def _gmlp_kernel(x_ref, mod_ref, ng_ref, win_ref, vg_ref, vb_ref, ws_ref, bs_ref, wout_ref, fg_ref,
                 o_ref, v_scr, ug_scr, y_scr, *, final_norm):
    tm, d = x_ref.shape[1], x_ref.shape[2]
    width = wout_ref.shape[0]
    groups = ws_ref.shape[0]
    gw = width // groups
    n_chunks = tm // A_CHUNK
    x = x_ref[0]
    shift, scale, gate = _split_mod(mod_ref[0], d)
    hb = _mod_norm(x, ng_ref[...], scale, shift).astype(BF16)

    def lane_tile_sum(t):
        return functools.reduce(jnp.add, [t[:, c:c + LANES] for c in range(0, gw, LANES)])

    def gated_u(g):
        lo = g * gw
        u = _bdot(hb, win_ref[:, lo:lo + gw])
        gt = _bdot(hb, win_ref[:, 2 * width + lo:2 * width + lo + gw])
        ug_scr[:, lo:lo + gw] = _gelu_times_silu(u, gt)

    row_sum = None
    for g in range(groups):
        lo = g * gw
        vg = _gelu(_bdot(hb, win_ref[:, width + lo:width + lo + gw]))
        v_scr[:, lo:lo + gw] = vg
        row_sum = lane_tile_sum(vg) if row_sum is None else row_sum + lane_tile_sum(vg)
    mu = jnp.sum(row_sum, axis=-1, keepdims=True) * (1.0 / width)
    sq_sum = None
    for g in range(groups):
        cen = v_scr[:, g * gw:(g + 1) * gw] - mu
        sq_sum = lane_tile_sum(cen * cen) if sq_sum is None else sq_sum + lane_tile_sum(cen * cen)
    rstd = lax.rsqrt(jnp.sum(sq_sum, axis=-1, keepdims=True) * (1.0 / width) + NORM_EPS)

    for g in range(GMLP_LOOKAHEAD):
        gated_u(g)
    for g in range(groups):
        lo = g * gw
        if g + GMLP_LOOKAHEAD < groups:
            gated_u(g + GMLP_LOOKAHEAD)
        vn = ((v_scr[:, lo:lo + gw] - mu) * rstd * vg_ref[:, lo:lo + gw]
              + vb_ref[:, lo:lo + gw]).astype(BF16)
        vn_wide = jnp.concatenate([vn[c * A_CHUNK:(c + 1) * A_CHUNK] for c in range(n_chunks)], axis=1)
        s_wide = _bdot(ws_ref[g], vn_wide) + bs_ref[:, g:g + 1]
        s = jnp.concatenate([s_wide[:, c * gw:(c + 1) * gw] for c in range(n_chunks)], axis=0)
        y_scr[:, lo:lo + gw] = (ug_scr[:, lo:lo + gw] * s).astype(BF16)

    out = x + gate * _bdot(y_scr[...], wout_ref[...])
    if final_norm:
        out = _rms(out, fg_ref[...])
    o_ref[0] = out


def _gmlp_layer(x, mod, ng, w_in, vn_g, vn_b, w_s, b_s, w_out, final_g, *, final_norm, tm):
    bn, seq, d = x.shape
    width = w_out.shape[0]
    groups = w_s.shape[0]
    return pl.pallas_call(
        functools.partial(_gmlp_kernel, final_norm=final_norm),
        grid=(bn, seq // tm),
        in_specs=[
            pl.BlockSpec((1, tm, d), lambda b, i: (b, i, 0)),
            _mod_spec(mod),
            _const_spec((1, d)),
            _const_spec((d, 3 * width)),
            _const_spec((1, width)),
            _const_spec((1, width)),
            _const_spec((groups, A_CHUNK, A_CHUNK)),
            _const_spec((A_CHUNK, groups)),
            _const_spec((width, d)),
            _const_spec((1, d)),
        ],
        out_specs=pl.BlockSpec((1, tm, d), lambda b, i: (b, i, 0)),
        out_shape=jax.ShapeDtypeStruct(x.shape, F32),
        scratch_shapes=[pltpu.VMEM((tm, width), F32), pltpu.VMEM((tm, width), F32),
                        pltpu.VMEM((tm, width), BF16)],
        compiler_params=_params(2),
        name="gmlp_layer",
    )(x, mod, ng.reshape(1, d), w_in.astype(BF16), vn_g.reshape(1, width), vn_b.reshape(1, width),
      w_s.astype(BF16), b_s.T, w_out.astype(BF16), final_g.reshape(1, d))


def _pool_kernel(xm_ref, xp_ref, xn_ref, mod_ref, ng_ref, win_ref, wgrp_ref, sc_ref, wout_ref,
                 o_ref, p_scr, y_scr, *, seq):
    tm, d = xm_ref.shape[1], xm_ref.shape[2]
    width = wout_ref.shape[0]
    gw = width // len(B_WINDOWS)
    i = pl.program_id(1)
    x = xm_ref[0]
    shift, scale, gate = _split_mod(mod_ref[0], d)
    xe = jnp.concatenate([xp_ref[0], x, xn_ref[0]], axis=0)
    he = _mod_norm(xe, ng_ref[...], scale, shift)

    pos_e = i * tm - B_HALO + lax.broadcasted_iota(jnp.int32, (tm + 2 * B_HALO, 1), 0)
    in_seq = (pos_e >= 0) & (pos_e < seq)
    p_scr[...] = jnp.where(in_seq, _bdot(he.astype(BF16), win_ref[:, :width]), 0.0)
    gt = _bdot(he[B_HALO:B_HALO + tm].astype(BF16), win_ref[:, width:])

    pos = i * tm + lax.broadcasted_iota(jnp.int32, (tm, 1), 0)
    for g, w in enumerate(B_WINDOWS):
        lo = g * gw
        acc = p_scr[B_HALO - w // 2:B_HALO - w // 2 + tm, lo:lo + gw]
        for k in range(-w // 2 + 1, w // 2):
            acc = acc + p_scr[B_HALO + k:B_HALO + k + tm, lo:lo + gw]
        cnt = jnp.minimum(pos + w // 2, seq) - jnp.maximum(pos - w // 2, 0)
        pooled = acc * (1.0 / cnt.astype(F32)) - p_scr[B_HALO:B_HALO + tm, lo:lo + gw]
        mixed = _bdot(pooled.astype(BF16), wgrp_ref[g]) * sc_ref[:, lo:lo + gw]
        y_scr[:, lo:lo + gw] = (mixed * jax.nn.silu(gt[:, lo:lo + gw])).astype(BF16)

    o_ref[0] = x + gate * _bdot(y_scr[...], wout_ref[...])


def _pool_layer(x, mod, ng, w_in, w_grp, scale, w_out, *, tm):
    bn, seq, d = x.shape
    width = w_out.shape[0]
    n_grp, gw, _ = w_grp.shape
    per_tile = tm // B_HALO
    n_halo_blocks = seq // B_HALO
    return pl.pallas_call(
        functools.partial(_pool_kernel, seq=seq),
        grid=(bn, seq // tm),
        in_specs=[
            pl.BlockSpec((1, tm, d), lambda b, i: (b, i, 0)),
            pl.BlockSpec((1, B_HALO, d), lambda b, i: (b, jnp.maximum(i * per_tile - 1, 0), 0)),
            pl.BlockSpec((1, B_HALO, d),
                         lambda b, i: (b, jnp.minimum((i + 1) * per_tile, n_halo_blocks - 1), 0)),
            _mod_spec(mod),
            _const_spec((1, d)),
            _const_spec((d, 2 * width)),
            _const_spec((n_grp, gw, gw)),
            _const_spec((1, width)),
            _const_spec((width, d)),
        ],
        out_specs=pl.BlockSpec((1, tm, d), lambda b, i: (b, i, 0)),
        out_shape=jax.ShapeDtypeStruct(x.shape, F32),
        scratch_shapes=[pltpu.VMEM((tm + 2 * B_HALO, width), F32), pltpu.VMEM((tm, width), BF16)],
        compiler_params=_params(2),
        name="pool_layer",
    )(x, x, x, mod, ng.reshape(1, d), w_in.astype(BF16), w_grp.astype(BF16),
      scale.reshape(1, width), w_out.astype(BF16))


def _rope_tables(seq):
    rows = seq // GRID_W
    row = jnp.repeat(jnp.arange(rows), GRID_W).astype(F32)
    col = jnp.tile(jnp.arange(GRID_W), rows).astype(F32)
    n_freq = HEAD_DIM // 4
    inv = ROPE_THETA ** (-jnp.arange(n_freq, dtype=F32) / n_freq)
    ang = jnp.concatenate([row[:, None] * inv, col[:, None] * inv], axis=-1)
    cos, sin = jnp.cos(ang), jnp.sin(ang)
    reps = LANES // HEAD_DIM
    cos_t = jnp.tile(jnp.concatenate([cos, cos], axis=-1), (1, reps))
    sin_t = jnp.tile(jnp.concatenate([-sin, sin], axis=-1), (1, reps))
    q_scale = LOG2E * HEAD_DIM ** -0.5
    return cos_t * q_scale, sin_t * q_scale, cos_t, sin_t


def _rope_slab(t, cos, sin, first_half):
    half = HEAD_DIM // 2
    partner = jnp.where(first_half, pltpu.roll(t, LANES - half, axis=1), pltpu.roll(t, half, axis=1))
    return t * cos + partner * sin


def _qkvg_kernel(x_ref, mod_ref, ng_ref, win_ref, cosq_ref, sinq_ref, cos_ref, sin_ref,
                 q_ref, kt_ref, v_ref, sg_ref):
    d = x_ref.shape[2]
    qw, kvw = q_ref.shape[2], kt_ref.shape[1]
    shift, scale, _ = _split_mod(mod_ref[0], d)
    hb = _mod_norm(x_ref[0], ng_ref[...], scale, shift).astype(BF16)
    lane = lax.broadcasted_iota(jnp.int32, (1, LANES), 1)
    first_half = (lane % HEAD_DIM) < HEAD_DIM // 2

    q = _bdot(hb, win_ref[:, :qw])
    for j in range(qw // LANES):
        sl = slice(j * LANES, (j + 1) * LANES)
        q_ref[0, :, sl] = _rope_slab(q[:, sl], cosq_ref[...], sinq_ref[...], first_half).astype(BF16)
    k = _bdot(hb, win_ref[:, qw:qw + kvw])
    for j in range(kvw // LANES):
        sl = slice(j * LANES, (j + 1) * LANES)
        kt_ref[0, sl, :] = _rope_slab(k[:, sl], cos_ref[...], sin_ref[...], first_half).T.astype(BF16)
    v_ref[0] = _bdot(hb, win_ref[:, qw + kvw:qw + 2 * kvw]).astype(BF16)
    sg_ref[0] = jax.nn.silu(_bdot(hb, win_ref[:, qw + 2 * kvw:])).astype(BF16)


def _qkvg_proj(x, mod, ng, w_in, rope, *, tm):
    bn, seq, d = x.shape
    kvw = KV_HEADS * HEAD_DIM
    qw = Q_GROUP * kvw
    tok = lambda w: pl.BlockSpec((1, tm, w), lambda b, i: (b, i, 0))
    table = pl.BlockSpec((tm, LANES), lambda b, i: (i, 0))
    return pl.pallas_call(
        _qkvg_kernel,
        grid=(bn, seq // tm),
        in_specs=[
            tok(d),
            _mod_spec(mod),
            _const_spec((1, d)),
            _const_spec((d, 2 * qw + 2 * kvw)),
            table, table, table, table,
        ],
        out_specs=[tok(qw), pl.BlockSpec((1, kvw, tm), lambda b, i: (b, 0, i)), tok(kvw), tok(qw)],
        out_shape=[jax.ShapeDtypeStruct((bn, seq, qw), BF16),
                   jax.ShapeDtypeStruct((bn, kvw, seq), BF16),
                   jax.ShapeDtypeStruct((bn, seq, kvw), BF16),
                   jax.ShapeDtypeStruct((bn, seq, qw), BF16)],
        compiler_params=_params(2),
        name="attn_qkvg_proj",
    )(x, mod, ng.reshape(1, d), w_in, *rope)


def _ctx_kv_kernel(x_ref, mod_ref, ng_ref, wkv_ref, kt_ref, v_ref):
    d = x_ref.shape[2]
    kvw = v_ref.shape[2]
    shift, scale, _ = _split_mod(mod_ref[0], d)
    hb = _mod_norm(x_ref[0], ng_ref[...], scale, shift).astype(BF16)
    kv = _bdot(hb, wkv_ref[...])
    kt_ref[0] = kv[:, :kvw].T.astype(BF16)
    v_ref[0] = kv[:, kvw:].astype(BF16)


def _ctx_kv_proj(ctx, mod, ng, w_kv):
    bn, lc, d = ctx.shape
    kvw = w_kv.shape[1] // 2
    tok = lambda w: pl.BlockSpec((1, lc, w), lambda b, i: (b, i, 0))
    return pl.pallas_call(
        _ctx_kv_kernel,
        grid=(bn, 1),
        in_specs=[tok(d), _mod_spec(mod), _const_spec((1, d)), _const_spec((d, 2 * kvw))],
        out_specs=[pl.BlockSpec((1, kvw, lc), lambda b, i: (b, 0, i)), tok(kvw)],
        out_shape=[jax.ShapeDtypeStruct((bn, kvw, lc), BF16), jax.ShapeDtypeStruct((bn, lc, kvw), BF16)],
        compiler_params=_params(2),
        name="attn_ctx_kv_proj",
    )(ctx, mod, ng.reshape(1, d), w_kv)


def _attn_kernel(sink_ref, x_ref, mod_ref, q_ref, kp_ref, km_ref, kn_ref, vp_ref, vm_ref, vn_ref,
                 kc_ref, vc_ref, sg_ref, wout_ref, o_ref,
                 k_scr, v_scr, vc_scr, s_scr, p_scr, r_scr, og_scr, *, seq):
    tq, d = x_ref.shape[1], x_ref.shape[2]
    kvw = vm_ref.shape[2]
    lc = vc_ref.shape[1]
    span = ATT_BLOCK + 2 * ATT_WINDOW
    rows = Q_GROUP * ATT_BLOCK
    n_blk = tq // ATT_BLOCK
    i = pl.program_id(1)
    neg = jnp.finfo(F32).min

    k_scr[:, 0:ATT_WINDOW] = kp_ref[0]
    k_scr[:, ATT_WINDOW:ATT_WINDOW + tq] = km_ref[0]
    k_scr[:, ATT_WINDOW + tq:] = kn_ref[0]
    ones = jnp.ones((tq + 2 * ATT_WINDOW, LANES), BF16)
    for t in range(kvw // LANES):
        src, dst = slice(t * LANES, (t + 1) * LANES), slice(2 * t * LANES, (2 * t + 1) * LANES)
        v_scr[0:ATT_WINDOW, dst] = vp_ref[0, :, src]
        v_scr[ATT_WINDOW:ATT_WINDOW + tq, dst] = vm_ref[0, :, src]
        v_scr[ATT_WINDOW + tq:, dst] = vn_ref[0, :, src]
        vc_scr[:, dst] = vc_ref[0, :, src]
        v_scr[:, (2 * t + 1) * LANES:(2 * t + 2) * LANES] = ones
        vc_scr[:, (2 * t + 1) * LANES:(2 * t + 2) * LANES] = ones[:lc]

    head_of_lane = lax.broadcasted_iota(jnp.int32, (1, kvw), 1) // HEAD_DIM
    lane = lax.broadcasted_iota(jnp.int32, (1, LANES), 1)
    low_head = lane < HEAD_DIM

    def softmax_rows(kh, r, sink, off_prev, off_next):
        sl = slice(r * SM_ROWS, (r + 1) * SM_ROWS)
        qi = (r * SM_ROWS) % ATT_BLOCK + lax.broadcasted_iota(jnp.int32, (SM_ROWS, 1), 0)
        parts = [s_scr[kh, sl, c * LANES:(c + 1) * LANES] for c in range((span + lc) // LANES)]
        parts[0] = jnp.where(lane >= qi + off_prev, parts[0], neg)
        last = span // LANES - 1
        parts[last] = jnp.where(lane <= qi - off_next, parts[last], neg)
        m = functools.reduce(jnp.maximum, parts)
        m = jnp.maximum(jnp.max(m, axis=-1, keepdims=True), sink)
        for c, part in enumerate(parts):
            p_scr[kh, sl, c * LANES:(c + 1) * LANES] = jnp.exp2(part - m).astype(BF16)
        r_scr[kh, sl, :] = jnp.broadcast_to(jnp.exp2(sink - m), (SM_ROWS, LANES))

    heads_per_tile = LANES // HEAD_DIM
    off_first = jnp.where(i > 0, 0, ATT_BLOCK)
    off_last = jnp.where(i < seq // tq - 1, 0, ATT_BLOCK)

    def scores(j, kh):
        own = head_of_lane == kh
        q_rows = slice(j * ATT_BLOCK, (j + 1) * ATT_BLOCK)
        lhs = jnp.concatenate(
            [jnp.where(own, q_ref[0, q_rows, g * kvw:(g + 1) * kvw], jnp.zeros((), BF16))
             for g in range(Q_GROUP)], axis=0)
        s_scr[kh, :, :span] = _bdot(lhs, k_scr[:, j * ATT_BLOCK:j * ATT_BLOCK + span])
        s_scr[kh, :, span:] = _bdot(lhs, kc_ref[0])

    items = [(j, kh) for j in range(n_blk) for kh in range(KV_HEADS)]
    for item in items[:ATT_LOOKAHEAD]:
        scores(*item)
    pv_pair = []
    for idx, (j, kh) in enumerate(items):
        t = kh // heads_per_tile
        if idx + ATT_LOOKAHEAD < len(items):
            scores(*items[idx + ATT_LOOKAHEAD])
        off_prev = off_first if j == 0 else 0
        off_next = off_last if j == n_blk - 1 else 0
        for r in range(rows // SM_ROWS):
            g = r * SM_ROWS // ATT_BLOCK
            softmax_rows(kh, r, sink_ref[kh * Q_GROUP + g] * LOG2E, off_prev, off_next)
        v_cols = slice(2 * t * LANES, (2 * t + 2) * LANES)
        pv = (_bdot(p_scr[kh, :, :span], v_scr[j * ATT_BLOCK:j * ATT_BLOCK + span, v_cols])
              + _bdot(p_scr[kh, :, span:], vc_scr[:, v_cols]))
        den = pv[:, LANES:] + r_scr[kh]
        pv_pair.append(pv[:, :LANES] * (1.0 / den))
        if len(pv_pair) == heads_per_tile:
            o_t = jnp.where(low_head, pv_pair[0], pv_pair[1])
            pv_pair = []
            q_rows = slice(j * ATT_BLOCK, (j + 1) * ATT_BLOCK)
            for g in range(Q_GROUP):
                cols = slice(g * kvw + t * LANES, g * kvw + (t + 1) * LANES)
                sg = sg_ref[0, q_rows, cols].astype(F32)
                og_scr[q_rows, cols] = (o_t[g * ATT_BLOCK:(g + 1) * ATT_BLOCK] * sg).astype(BF16)

    _, _, gate = _split_mod(mod_ref[0], d)
    o_ref[0] = x_ref[0] + gate * _bdot(og_scr[...], wout_ref[...])


def _attn_layer(x, mod, q, kt, v, kct, vc, sg, sink, w_out, *, tq):
    bn, seq, d = x.shape
    kvw = v.shape[2]
    qw = q.shape[2]
    lc = vc.shape[1]
    nb = seq // ATT_BLOCK
    per_tile = tq // ATT_BLOCK
    span = ATT_BLOCK + 2 * ATT_WINDOW
    rows = Q_GROUP * ATT_BLOCK
    prev_blk = lambda i: jnp.maximum(i * per_tile - 1, 0)
    next_blk = lambda i: jnp.minimum((i + 1) * per_tile, nb - 1)
    tok = lambda w: pl.BlockSpec((1, tq, w), lambda b, i: (b, i, 0))
    prev = pl.BlockSpec((1, ATT_BLOCK, kvw), lambda b, i: (b, prev_blk(i), 0))
    nxt = pl.BlockSpec((1, ATT_BLOCK, kvw), lambda b, i: (b, next_blk(i), 0))
    prev_t = pl.BlockSpec((1, kvw, ATT_BLOCK), lambda b, i: (b, 0, prev_blk(i)))
    main_t = pl.BlockSpec((1, kvw, tq), lambda b, i: (b, 0, i))
    next_t = pl.BlockSpec((1, kvw, ATT_BLOCK), lambda b, i: (b, 0, next_blk(i)))
    return pl.pallas_call(
        functools.partial(_attn_kernel, seq=seq),
        grid=(bn, seq // tq),
        in_specs=[
            pl.BlockSpec(memory_space=pltpu.SMEM),
            tok(d), _mod_spec(mod), tok(qw),
            prev_t, main_t, next_t, prev, tok(kvw), nxt,
            pl.BlockSpec((1, kvw, lc), lambda b, i: (b, 0, 0)),
            pl.BlockSpec((1, lc, kvw), lambda b, i: (b, 0, 0)),
            tok(qw), _const_spec((qw, d)),
        ],
        out_specs=tok(d),
        out_shape=jax.ShapeDtypeStruct(x.shape, F32),
        scratch_shapes=[
            pltpu.VMEM((kvw, tq + 2 * ATT_WINDOW), BF16),
            pltpu.VMEM((tq + 2 * ATT_WINDOW, 2 * kvw), BF16),
            pltpu.VMEM((lc, 2 * kvw), BF16),
            pltpu.VMEM((KV_HEADS, rows, span + lc), F32),
            pltpu.VMEM((KV_HEADS, rows, span + lc), BF16),
            pltpu.VMEM((KV_HEADS, rows, LANES), F32),
            pltpu.VMEM((tq, qw), BF16),
        ],
        compiler_params=_params(2),
        name="attn_layer",
    )(sink, x, mod, q, kt, kt, kt, v, v, v, kct, vc, sg, w_out)


def _group_major_cols(w):
    lead = w.shape[:-1]
    w = w.reshape(lead + (KV_HEADS, Q_GROUP, HEAD_DIM))
    return jnp.swapaxes(w, -3, -2).reshape(lead + (KV_HEADS * Q_GROUP * HEAD_DIM,))


def kernel(x, c, ctx, c_ctx, ada_w, ada_b, norm_g, final_g, gmlp_w_in, gmlp_vnorm_g, gmlp_vnorm_b,
           gmlp_w_s, gmlp_b_s, gmlp_w_out, pool_w_in, pool_w_grp, pool_scale, pool_w_out,
           attn_w_in, attn_sink, attn_w_out):
    bn, seq, d = x.shape
    depth = ada_w.shape[0]
    lc = ctx.shape[1]
    n_mixers = 3
    kvw = KV_HEADS * HEAD_DIM
    qw = Q_GROUP * kvw

    pad = (-(bn + 1)) % COND_ROWS_ALIGN
    c_all = jnp.concatenate([c, c_ctx[None], jnp.zeros((pad, d), c.dtype)], axis=0)
    mod = _modulation(c_all, ada_w, ada_b)
    rope = _rope_tables(seq)

    for i in range(depth):
        kind, j = i % n_mixers, i // n_mixers
        ctx_out = any(l % n_mixers == 2 for l in range(i + 1, depth))
        last = i == depth - 1
        mod_lat = mod[i, :bn].reshape(bn, 1, 3 * d)
        mod_ctx = mod[i, bn:bn + 1].reshape(1, 1, 3 * d)
        if kind == 0:
            run = functools.partial(
                _gmlp_layer, ng=norm_g[i], w_in=gmlp_w_in[j], vn_g=gmlp_vnorm_g[j],
                vn_b=gmlp_vnorm_b[j], w_s=gmlp_w_s[j], b_s=gmlp_b_s[j], w_out=gmlp_w_out[j],
                final_g=final_g, tm=TOKEN_TILE)
            x = run(x, mod_lat, final_norm=last)
            if ctx_out:
                ctx = run(ctx.reshape(1, bn * lc, d), mod_ctx, final_norm=False).reshape(bn, lc, d)
        elif kind == 1:
            run = functools.partial(
                _pool_layer, ng=norm_g[i], w_in=pool_w_in[j], w_grp=pool_w_grp[j],
                scale=pool_scale[j], w_out=pool_w_out[j])
            x = run(x, mod_lat, tm=TOKEN_TILE)
            if ctx_out:
                ctx = run(ctx, mod_ctx, tm=min(TOKEN_TILE, lc))
        else:
            assert not ctx_out, "context-stream attention output is not needed by this depth"
            w_in = attn_w_in[j]
            w_q, w_kv, w_g = w_in[:, :qw], w_in[:, qw:qw + 2 * kvw], w_in[:, qw + 2 * kvw:]
            w_lat = jnp.concatenate([_group_major_cols(w_q), w_kv, _group_major_cols(w_g)],
                                    axis=1).astype(BF16)
            w_out = _group_major_cols(attn_w_out[j].T).T.astype(BF16)
            q, k, v, sg = _qkvg_proj(x, mod_lat, norm_g[i], w_lat, rope, tm=QKV_TILE)
            kc, vc = _ctx_kv_proj(ctx, mod_ctx, norm_g[i], w_kv.astype(BF16))
            x = _attn_layer(x, mod_lat, q, k, v, kc, vc, sg, attn_sink[j], w_out, tq=TOKEN_TILE)
    if depth % n_mixers != 1:
        raise NotImplementedError("final RMSNorm is fused into a trailing gMLP layer")
    return x
```

```python
import functools

import jax
import jax.numpy as jnp
from jax import lax
from jax.experimental import pallas as pl
from jax.experimental.pallas import tpu as pltpu

NORM_EPS = 1e-6
GRID_W = 64
A_CHUNK = 128
B_WINDOWS = (2, 4, 8, 16)
B_HALO = 8
HEAD_DIM = 64
KV_HEADS = 4
Q_GROUP = 4
ATT_WINDOW = 128
ATT_BLOCK = 128
ROPE_THETA = 10000.0
LOG2E = 1.4426950408889634
SM_ROWS = 32
GMLP_LOOKAHEAD = 2
ATT_LOOKAHEAD = 1
LANES = 128
COND_ROWS_ALIGN = 8

F32 = jnp.float32
BF16 = jnp.bfloat16

VMEM_LIMIT_BYTES = 56 * 1024 * 1024
TOKEN_TILE = 512
QKV_TILE = 1024


def _const_spec(shape):
    zeros = (0,) * len(shape)
    return pl.BlockSpec(shape, lambda *_: zeros, pipeline_mode=pl.Buffered(1))


def _params(n_axes):
    return pltpu.CompilerParams(
        dimension_semantics=("parallel",) * n_axes,
        vmem_limit_bytes=VMEM_LIMIT_BYTES,
    )


def _rms(x, g):
    return x * lax.rsqrt(jnp.mean(x * x, axis=-1, keepdims=True) + NORM_EPS) * g


def _mod_norm(x, g, scale, shift):
    gain = g * (1.0 + scale)
    return x * lax.rsqrt(jnp.mean(x * x, axis=-1, keepdims=True) + NORM_EPS) * gain + shift


def _split_mod(mod, d):
    return mod[:, :d], mod[:, d:2 * d], mod[:, 2 * d:]


GELU_A = -2.0 * 0.7978845608028654 * LOG2E
GELU_B = GELU_A * 0.044715


def _gelu_exp(x):
    return jnp.exp2(x * (GELU_A + GELU_B * (x * x)))


def _gelu(x):
    return x / (1.0 + _gelu_exp(x))


def _gelu_times_silu(u, t):
    return (u * t) / ((1.0 + _gelu_exp(u)) * (1.0 + jnp.exp2(t * -LOG2E)))


def _bdot(a, b):
    return jnp.dot(a, b, preferred_element_type=F32)


def _mod_kernel(c_ref, w_ref, b_ref, o_ref):
    cond = jax.nn.silu(c_ref[...])
    o_ref[0] = jnp.dot(cond, w_ref[0], preferred_element_type=F32,
                       precision=lax.Precision.HIGHEST) + b_ref[0]


def _modulation(c_all, ada_w, ada_b):
    depth, d, d3 = ada_w.shape
    rows = c_all.shape[0]
    tn = d
    return pl.pallas_call(
        _mod_kernel,
        grid=(depth, d3 // tn),
        in_specs=[
            pl.BlockSpec((rows, d), lambda i, j: (0, 0)),
            pl.BlockSpec((1, d, tn), lambda i, j: (i, 0, j)),
            pl.BlockSpec((1, 1, tn), lambda i, j: (i, 0, j)),
        ],
        out_specs=pl.BlockSpec((1, rows, tn), lambda i, j: (i, 0, j)),
        out_shape=jax.ShapeDtypeStruct((depth, rows, d3), F32),
        compiler_params=_params(2),
        name="adaln_modulation",
    )(c_all, ada_w, ada_b.reshape(depth, 1, d3))


def _mod_spec(mod):
    if mod.shape[0] == 1:
        return pl.BlockSpec((1, 1, mod.shape[2]), lambda b, i: (0, 0, 0))
    return pl.BlockSpec((1, 1, mod.shape[2]), lambda b, i: (b, 0, 0))


def _gmlp_kernel(x_ref, mod_ref, ng_ref, win_ref, vg_ref, vb_ref, ws_ref, bs_ref, wout_ref, fg_ref,
                 o_ref, v_scr, ug_scr, y_scr, *, final_norm):
    tm, d = x_ref.shape[1], x_ref.shape[2]
    width = wout_ref.shape[0]
    groups = ws_ref.shape[0]
    gw = width // groups
    n_chunks = tm // A_CHUNK
    x = x_ref[0]
    shift, scale, gate = _split_mod(mod_ref[0], d)
    hb = _mod_norm(x, ng_ref[...], scale, shift).astype(BF16)

    def lane_tile_sum(t):
        return functools.reduce(jnp.add, [t[:, c:c + LANES] for c in range(0, gw, LANES)])

    def gated_u(g):
        lo = g * gw
        u = _bdot(hb, win_ref[:, lo:lo + gw])
        gt = _bdot(hb, win_ref[:, 2 * width + lo:2 * width + lo + gw])
        ug_scr[:, lo:lo + gw] = _gelu_times_silu(u, gt)

    row_sum = None
    for g in range(groups):
        lo = g * gw
        vg = _gelu(_bdot(hb, win_ref[:, width + lo:width + lo + gw]))
        v_scr[:, lo:lo + gw] = vg
        row_sum = lane_tile_sum(vg) if row_sum is None else row_sum + lane_tile_sum(vg)
    mu = jnp.sum(row_sum, axis=-1, keepdims=True) * (1.0 / width)
    sq_sum = None
    for g in range(groups):
        cen = v_scr[:, g * gw:(g + 1) * gw] - mu
        sq_sum = lane_tile_sum(cen * cen) if sq_sum is None else sq_sum + lane_tile_sum(cen * cen)
    rstd = lax.rsqrt(jnp.sum(sq_sum, axis=-1, keepdims=True) * (1.0 / width) + NORM_EPS)

    for g in range(GMLP_LOOKAHEAD):
        gated_u(g)
    for g in range(groups):
        lo = g * gw
        if g + GMLP_LOOKAHEAD < groups:
            gated_u(g + GMLP_LOOKAHEAD)
        vn = ((v_scr[:, lo:lo + gw] - mu) * rstd * vg_ref[:, lo:lo + gw]
              + vb_ref[:, lo:lo + gw]).astype(BF16)
        bias = bs_ref[:, g:g + 1]
        s = jnp.concatenate(
            [_bdot(ws_ref[g], vn[c * A_CHUNK:(c + 1) * A_CHUNK]) + bias for c in range(n_chunks)],
            axis=0)
        y_scr[:, lo:lo + gw] = (ug_scr[:, lo:lo + gw] * s).astype(BF16)

    out = x + gate * _bdot(y_scr[...], wout_ref[...])
    if final_norm:
        out = _rms(out, fg_ref[...])
    o_ref[0] = out


def _gmlp_layer(x, mod, ng, w_in, vn_g, vn_b, w_s, b_s, w_out, final_g, *, final_norm, tm):
    bn, seq, d = x.shape
    width = w_out.shape[0]
    groups = w_s.shape[0]
    return pl.pallas_call(
        functools.partial(_gmlp_kernel, final_norm=final_norm),
        grid=(bn, seq // tm),
        in_specs=[
            pl.BlockSpec((1, tm, d), lambda b, i: (b, i, 0)),
            _mod_spec(mod),
            _const_spec((1, d)),
            _const_spec((d, 3 * width)),
            _const_spec((1, width)),
            _const_spec((1, width)),
            _const_spec((groups, A_CHUNK, A_CHUNK)),
            _const_spec((A_CHUNK, groups)),
            _const_spec((width, d)),
            _const_spec((1, d)),
        ],
        out_specs=pl.BlockSpec((1, tm, d), lambda b, i: (b, i, 0)),
        out_shape=jax.ShapeDtypeStruct(x.shape, F32),
        scratch_shapes=[pltpu.VMEM((tm, width), F32), pltpu.VMEM((tm, width), F32),
                        pltpu.VMEM((tm, width), BF16)],
        compiler_params=_params(2),
        name="gmlp_layer",
    )(x, mod, ng.reshape(1, d), w_in.astype(BF16), vn_g.reshape(1, width), vn_b.reshape(1, width),
      w_s.astype(BF16), b_s.T, w_out.astype(BF16), final_g.reshape(1, d))


def _pool_kernel(xm_ref, xp_ref, xn_ref, mod_ref, ng_ref, win_ref, wgrp_ref, sc_ref, wout_ref,
                 o_ref, p_scr, y_scr, *, seq):
    tm, d = xm_ref.shape[1], xm_ref.shape[2]
    width = wout_ref.shape[0]
    gw = width // len(B_WINDOWS)
    i = pl.program_id(1)
    x = xm_ref[0]
    shift, scale, gate = _split_mod(mod_ref[0], d)
    xe = jnp.concatenate([xp_ref[0], x, xn_ref[0]], axis=0)
    he = _mod_norm(xe, ng_ref[...], scale, shift)

    pos_e = i * tm - B_HALO + lax.broadcasted_iota(jnp.int32, (tm + 2 * B_HALO, 1), 0)
    in_seq = (pos_e >= 0) & (pos_e < seq)
    p_scr[...] = jnp.where(in_seq, _bdot(he.astype(BF16), win_ref[:, :width]), 0.0)
    gt = _bdot(he[B_HALO:B_HALO + tm].astype(BF16), win_ref[:, width:])

    pos = i * tm + lax.broadcasted_iota(jnp.int32, (tm, 1), 0)
    for g, w in enumerate(B_WINDOWS):
        lo = g * gw
        acc = p_scr[B_HALO - w // 2:B_HALO - w // 2 + tm, lo:lo + gw]
        for k in range(-w // 2 + 1, w // 2):
            acc = acc + p_scr[B_HALO + k:B_HALO + k + tm, lo:lo + gw]
        cnt = jnp.minimum(pos + w // 2, seq) - jnp.maximum(pos - w // 2, 0)
        pooled = acc * (1.0 / cnt.astype(F32)) - p_scr[B_HALO:B_HALO + tm, lo:lo + gw]
        mixed = _bdot(pooled.astype(BF16), wgrp_ref[g]) * sc_ref[:, lo:lo + gw]
        y_scr[:, lo:lo + gw] = (mixed * jax.nn.silu(gt[:, lo:lo + gw])).astype(BF16)

    o_ref[0] = x + gate * _bdot(y_scr[...], wout_ref[...])


def _pool_layer(x, mod, ng, w_in, w_grp, scale, w_out, *, tm):
    bn, seq, d = x.shape
    width = w_out.shape[0]
    n_grp, gw, _ = w_grp.shape
    per_tile = tm // B_HALO
    n_halo_blocks = seq // B_HALO
    return pl.pallas_call(
        functools.partial(_pool_kernel, seq=seq),
        grid=(bn, seq // tm),
        in_specs=[
            pl.BlockSpec((1, tm, d), lambda b, i: (b, i, 0)),
            pl.BlockSpec((1, B_HALO, d), lambda b, i: (b, jnp.maximum(i * per_tile - 1, 0), 0)),
            pl.BlockSpec((1, B_HALO, d),
                         lambda b, i: (b, jnp.minimum((i + 1) * per_tile, n_halo_blocks - 1), 0)),
            _mod_spec(mod),
            _const_spec((1, d)),
            _const_spec((d, 2 * width)),
            _const_spec((n_grp, gw, gw)),
            _const_spec((1, width)),
            _const_spec((width, d)),
        ],
        out_specs=pl.BlockSpec((1, tm, d), lambda b, i: (b, i, 0)),
        out_shape=jax.ShapeDtypeStruct(x.shape, F32),
        scratch_shapes=[pltpu.VMEM((tm + 2 * B_HALO, width), F32), pltpu.VMEM((tm, width), BF16)],
        compiler_params=_params(2),
        name="pool_layer",
    )(x, x, x, mod, ng.reshape(1, d), w_in.astype(BF16), w_grp.astype(BF16),
      scale.reshape(1, width), w_out.astype(BF16))


def _rope_tables(seq):
    rows = seq // GRID_W
    row = jnp.repeat(jnp.arange(rows), GRID_W).astype(F32)
    col = jnp.tile(jnp.arange(GRID_W), rows).astype(F32)
    n_freq = HEAD_DIM // 4
    inv = ROPE_THETA ** (-jnp.arange(n_freq, dtype=F32) / n_freq)
    ang = jnp.concatenate([row[:, None] * inv, col[:, None] * inv], axis=-1)
    cos, sin = jnp.cos(ang), jnp.sin(ang)
    reps = LANES // HEAD_DIM
    cos_t = jnp.tile(jnp.concatenate([cos, cos], axis=-1), (1, reps))
    sin_t = jnp.tile(jnp.concatenate([-sin, sin], axis=-1), (1, reps))
    q_scale = LOG2E * HEAD_DIM ** -0.5
    return cos_t * q_scale, sin_t * q_scale, cos_t, sin_t


def _rope_slab(t, cos, sin, first_half):
    half = HEAD_DIM // 2
    partner = jnp.where(first_half, pltpu.roll(t, LANES - half, axis=1), pltpu.roll(t, half, axis=1))
    return t * cos + partner * sin


def _qkvg_kernel(x_ref, mod_ref, ng_ref, win_ref, cosq_ref, sinq_ref, cos_ref, sin_ref,
                 q_ref, kt_ref, v_ref, sg_ref):
    d = x_ref.shape[2]
    qw, kvw = q_ref.shape[2], kt_ref.shape[1]
    shift, scale, _ = _split_mod(mod_ref[0], d)
    hb = _mod_norm(x_ref[0], ng_ref[...], scale, shift).astype(BF16)
    lane = lax.broadcasted_iota(jnp.int32, (1, LANES), 1)
    first_half = (lane % HEAD_DIM) < HEAD_DIM // 2

    q = _bdot(hb, win_ref[:, :qw])
    for j in range(qw // LANES):
        sl = slice(j * LANES, (j + 1) * LANES)
        q_ref[0, :, sl] = _rope_slab(q[:, sl], cosq_ref[...], sinq_ref[...], first_half).astype(BF16)
    k = _bdot(hb, win_ref[:, qw:qw + kvw])
    for j in range(kvw // LANES):
        sl = slice(j * LANES, (j + 1) * LANES)
        kt_ref[0, sl, :] = _rope_slab(k[:, sl], cos_ref[...], sin_ref[...], first_half).T.astype(BF16)
    v_ref[0] = _bdot(hb, win_ref[:, qw + kvw:qw + 2 * kvw]).astype(BF16)
    sg_ref[0] = jax.nn.silu(_bdot(hb, win_ref[:, qw + 2 * kvw:])).astype(BF16)


def _qkvg_proj(x, mod, ng, w_in, rope, *, tm):
    bn, seq, d = x.shape
    kvw = KV_HEADS * HEAD_DIM
    qw = Q_GROUP * kvw
    tok = lambda w: pl.BlockSpec((1, tm, w), lambda b, i: (b, i, 0))
    table = pl.BlockSpec((tm, LANES), lambda b, i: (i, 0))
    return pl.pallas_call(
        _qkvg_kernel,
        grid=(bn, seq // tm),
        in_specs=[
            tok(d),
            _mod_spec(mod),
            _const_spec((1, d)),
            _const_spec((d, 2 * qw + 2 * kvw)),
            table, table, table, table,
        ],
        out_specs=[tok(qw), pl.BlockSpec((1, kvw, tm), lambda b, i: (b, 0, i)), tok(kvw), tok(qw)],
        out_shape=[jax.ShapeDtypeStruct((bn, seq, qw), BF16),
                   jax.ShapeDtypeStruct((bn, kvw, seq), BF16),
                   jax.ShapeDtypeStruct((bn, seq, kvw), BF16),
                   jax.ShapeDtypeStruct((bn, seq, qw), BF16)],
        compiler_params=_params(2),
        name="attn_qkvg_proj",
    )(x, mod, ng.reshape(1, d), w_in, *rope)


def _ctx_kv_kernel(x_ref, mod_ref, ng_ref, wkv_ref, kt_ref, v_ref):
    d = x_ref.shape[2]
    kvw = v_ref.shape[2]
    shift, scale, _ = _split_mod(mod_ref[0], d)
    hb = _mod_norm(x_ref[0], ng_ref[...], scale, shift).astype(BF16)
    kv = _bdot(hb, wkv_ref[...])
    kt_ref[0] = kv[:, :kvw].T.astype(BF16)
    v_ref[0] = kv[:, kvw:].astype(BF16)


def _ctx_kv_proj(ctx, mod, ng, w_kv):
    bn, lc, d = ctx.shape
    kvw = w_kv.shape[1] // 2
    tok = lambda w: pl.BlockSpec((1, lc, w), lambda b, i: (b, i, 0))
    return pl.pallas_call(
        _ctx_kv_kernel,
        grid=(bn, 1),
        in_specs=[tok(d), _mod_spec(mod), _const_spec((1, d)), _const_spec((d, 2 * kvw))],
        out_specs=[pl.BlockSpec((1, kvw, lc), lambda b, i: (b, 0, i)), tok(kvw)],
        out_shape=[jax.ShapeDtypeStruct((bn, kvw, lc), BF16), jax.ShapeDtypeStruct((bn, lc, kvw), BF16)],
        compiler_params=_params(2),
        name="attn_ctx_kv_proj",
    )(ctx, mod, ng.reshape(1, d), w_kv)


def _attn_kernel(sink_ref, x_ref, mod_ref, q_ref, kp_ref, km_ref, kn_ref, vp_ref, vm_ref, vn_ref,
                 kc_ref, vc_ref, sg_ref, wout_ref, o_ref,
                 k_scr, v_scr, vc_scr, s_scr, p_scr, r_scr, og_scr, *, seq):
    tq, d = x_ref.shape[1], x_ref.shape[2]
    kvw = vm_ref.shape[2]
    lc = vc_ref.shape[1]
    span = ATT_BLOCK + 2 * ATT_WINDOW
    rows = Q_GROUP * ATT_BLOCK
    n_blk = tq // ATT_BLOCK
    i = pl.program_id(1)
    neg = jnp.finfo(F32).min

    k_scr[:, 0:ATT_WINDOW] = kp_ref[0]
    k_scr[:, ATT_WINDOW:ATT_WINDOW + tq] = km_ref[0]
    k_scr[:, ATT_WINDOW + tq:] = kn_ref[0]
    ones = jnp.ones((tq + 2 * ATT_WINDOW, LANES), BF16)
    for t in range(kvw // LANES):
        src, dst = slice(t * LANES, (t + 1) * LANES), slice(2 * t * LANES, (2 * t + 1) * LANES)
        v_scr[0:ATT_WINDOW, dst] = vp_ref[0, :, src]
        v_scr[ATT_WINDOW:ATT_WINDOW + tq, dst] = vm_ref[0, :, src]
        v_scr[ATT_WINDOW + tq:, dst] = vn_ref[0, :, src]
        vc_scr[:, dst] = vc_ref[0, :, src]
        v_scr[:, (2 * t + 1) * LANES:(2 * t + 2) * LANES] = ones
        vc_scr[:, (2 * t + 1) * LANES:(2 * t + 2) * LANES] = ones[:lc]

    head_of_lane = lax.broadcasted_iota(jnp.int32, (1, kvw), 1) // HEAD_DIM
    lane = lax.broadcasted_iota(jnp.int32, (1, LANES), 1)
    low_head = lane < HEAD_DIM

    def softmax_rows(kh, r, sink, off_prev, off_next):
        sl = slice(r * SM_ROWS, (r + 1) * SM_ROWS)
        qi = (r * SM_ROWS) % ATT_BLOCK + lax.broadcasted_iota(jnp.int32, (SM_ROWS, 1), 0)
        parts = [s_scr[kh, sl, c * LANES:(c + 1) * LANES] for c in range((span + lc) // LANES)]
        parts[0] = jnp.where(lane >= qi + off_prev, parts[0], neg)
        last = span // LANES - 1
        parts[last] = jnp.where(lane <= qi - off_next, parts[last], neg)
        m = functools.reduce(jnp.maximum, parts)
        m = jnp.maximum(jnp.max(m, axis=-1, keepdims=True), sink)
        for c, part in enumerate(parts):
            p_scr[kh, sl, c * LANES:(c + 1) * LANES] = jnp.exp2(part - m).astype(BF16)
        r_scr[kh, sl, :] = jnp.broadcast_to(jnp.exp2(sink - m), (SM_ROWS, LANES))

    heads_per_tile = LANES // HEAD_DIM
    off_first = jnp.where(i > 0, 0, ATT_BLOCK)
    off_last = jnp.where(i < seq // tq - 1, 0, ATT_BLOCK)

    def scores(j, kh):
        own = head_of_lane == kh
        q_rows = slice(j * ATT_BLOCK, (j + 1) * ATT_BLOCK)
        lhs = jnp.concatenate(
            [jnp.where(own, q_ref[0, q_rows, g * kvw:(g + 1) * kvw], jnp.zeros((), BF16))
             for g in range(Q_GROUP)], axis=0)
        s_scr[kh, :, :span] = _bdot(lhs, k_scr[:, j * ATT_BLOCK:j * ATT_BLOCK + span])
        s_scr[kh, :, span:] = _bdot(lhs, kc_ref[0])

    items = [(j, kh) for j in range(n_blk) for kh in range(KV_HEADS)]
    for item in items[:ATT_LOOKAHEAD]:
        scores(*item)
    pv_pair = []
    for idx, (j, kh) in enumerate(items):
        t = kh // heads_per_tile
        if idx + ATT_LOOKAHEAD < len(items):
            scores(*items[idx + ATT_LOOKAHEAD])
        off_prev = off_first if j == 0 else 0
        off_next = off_last if j == n_blk - 1 else 0
        for r in range(rows // SM_ROWS):
            g = r * SM_ROWS // ATT_BLOCK
            softmax_rows(kh, r, sink_ref[kh * Q_GROUP + g] * LOG2E, off_prev, off_next)
        v_cols = slice(2 * t * LANES, (2 * t + 2) * LANES)
        pv = (_bdot(p_scr[kh, :, :span], v_scr[j * ATT_BLOCK:j * ATT_BLOCK + span, v_cols])
              + _bdot(p_scr[kh, :, span:], vc_scr[:, v_cols]))
        den = pv[:, LANES:] + r_scr[kh]
        pv_pair.append(pv[:, :LANES] * (1.0 / den))
        if len(pv_pair) == heads_per_tile:
            o_t = jnp.where(low_head, pv_pair[0], pv_pair[1])
            pv_pair = []
            q_rows = slice(j * ATT_BLOCK, (j + 1) * ATT_BLOCK)
            for g in range(Q_GROUP):
                cols = slice(g * kvw + t * LANES, g * kvw + (t + 1) * LANES)
                sg = sg_ref[0, q_rows, cols].astype(F32)
                og_scr[q_rows, cols] = (o_t[g * ATT_BLOCK:(g + 1) * ATT_BLOCK] * sg).astype(BF16)

    _, _, gate = _split_mod(mod_ref[0], d)
    o_ref[0] = x_ref[0] + gate * _bdot(og_scr[...], wout_ref[...])


def _attn_layer(x, mod, q, kt, v, kct, vc, sg, sink, w_out, *, tq):
    bn, seq, d = x.shape
    kvw = v.shape[2]
    qw = q.shape[2]
    lc = vc.shape[1]
    nb = seq // ATT_BLOCK
    per_tile = tq // ATT_BLOCK
    span = ATT_BLOCK + 2 * ATT_WINDOW
    rows = Q_GROUP * ATT_BLOCK
    prev_blk = lambda i: jnp.maximum(i * per_tile - 1, 0)
    next_blk = lambda i: jnp.minimum((i + 1) * per_tile, nb - 1)
    tok = lambda w: pl.BlockSpec((1, tq, w), lambda b, i: (b, i, 0))
    prev = pl.BlockSpec((1, ATT_BLOCK, kvw), lambda b, i: (b, prev_blk(i), 0))
    nxt = pl.BlockSpec((1, ATT_BLOCK, kvw), lambda b, i: (b, next_blk(i), 0))
    prev_t = pl.BlockSpec((1, kvw, ATT_BLOCK), lambda b, i: (b, 0, prev_blk(i)))
    main_t = pl.BlockSpec((1, kvw, tq), lambda b, i: (b, 0, i))
    next_t = pl.BlockSpec((1, kvw, ATT_BLOCK), lambda b, i: (b, 0, next_blk(i)))
    return pl.pallas_call(
        functools.partial(_attn_kernel, seq=seq),
        grid=(bn, seq // tq),
        in_specs=[
            pl.BlockSpec(memory_space=pltpu.SMEM),
            tok(d), _mod_spec(mod), tok(qw),
            prev_t, main_t, next_t, prev, tok(kvw), nxt,
            pl.BlockSpec((1, kvw, lc), lambda b, i: (b, 0, 0)),
            pl.BlockSpec((1, lc, kvw), lambda b, i: (b, 0, 0)),
            tok(qw), _const_spec((qw, d)),
        ],
        out_specs=tok(d),
        out_shape=jax.ShapeDtypeStruct(x.shape, F32),
        scratch_shapes=[
            pltpu.VMEM((kvw, tq + 2 * ATT_WINDOW), BF16),
            pltpu.VMEM((tq + 2 * ATT_WINDOW, 2 * kvw), BF16),
            pltpu.VMEM((lc, 2 * kvw), BF16),
            pltpu.VMEM((KV_HEADS, rows, span + lc), F32),
            pltpu.VMEM((KV_HEADS, rows, span + lc), BF16),
            pltpu.VMEM((KV_HEADS, rows, LANES), F32),
            pltpu.VMEM((tq, qw), BF16),
        ],
        compiler_params=_params(2),
        name="attn_layer",
    )(sink, x, mod, q, kt, kt, kt, v, v, v, kct, vc, sg, w_out)


def _group_major_cols(w):
    lead = w.shape[:-1]
    w = w.reshape(lead + (KV_HEADS, Q_GROUP, HEAD_DIM))
    return jnp.swapaxes(w, -3, -2).reshape(lead + (KV_HEADS * Q_GROUP * HEAD_DIM,))


def kernel(x, c, ctx, c_ctx, ada_w, ada_b, norm_g, final_g, gmlp_w_in, gmlp_vnorm_g, gmlp_vnorm_b,
           gmlp_w_s, gmlp_b_s, gmlp_w_out, pool_w_in, pool_w_grp, pool_scale, pool_w_out,
           attn_w_in, attn_sink, attn_w_out):
    bn, seq, d = x.shape
    depth = ada_w.shape[0]
    lc = ctx.shape[1]
    n_mixers = 3
    kvw = KV_HEADS * HEAD_DIM
    qw = Q_GROUP * kvw

    pad = (-(bn + 1)) % COND_ROWS_ALIGN
    c_all = jnp.concatenate([c, c_ctx[None], jnp.zeros((pad, d), c.dtype)], axis=0)
    mod = _modulation(c_all, ada_w, ada_b)
    rope = _rope_tables(seq)

    for i in range(depth):
        kind, j = i % n_mixers, i // n_mixers
        ctx_out = any(l % n_mixers == 2 for l in range(i + 1, depth))
        last = i == depth - 1
        mod_lat = mod[i, :bn].reshape(bn, 1, 3 * d)
        mod_ctx = mod[i, bn:bn + 1].reshape(1, 1, 3 * d)
        if kind == 0:
            run = functools.partial(
                _gmlp_layer, ng=norm_g[i], w_in=gmlp_w_in[j], vn_g=gmlp_vnorm_g[j],
                vn_b=gmlp_vnorm_b[j], w_s=gmlp_w_s[j], b_s=gmlp_b_s[j], w_out=gmlp_w_out[j],
                final_g=final_g, tm=TOKEN_TILE)
            x = run(x, mod_lat, final_norm=last)
            if ctx_out:
                ctx = run(ctx.reshape(1, bn * lc, d), mod_ctx, final_norm=False).reshape(bn, lc, d)
        elif kind == 1:
            run = functools.partial(
                _pool_layer, ng=norm_g[i], w_in=pool_w_in[j], w_grp=pool_w_grp[j],
                scale=pool_scale[j], w_out=pool_w_out[j])
            x = run(x, mod_lat, tm=TOKEN_TILE)
            if ctx_out:
                ctx = run(ctx, mod_ctx, tm=min(TOKEN_TILE, lc))
        else:
            assert not ctx_out, "context-stream attention output is not needed by this depth"
            w_in = attn_w_in[j]
            w_q, w_kv, w_g = w_in[:, :qw], w_in[:, qw:qw + 2 * kvw], w_in[:, qw + 2 * kvw:]
            w_lat = jnp.concatenate([_group_major_cols(w_q), w_kv, _group_major_cols(w_g)],
                                    axis=1).astype(BF16)
            w_out = _group_major_cols(attn_w_out[j].T).T.astype(BF16)
            q, k, v, sg = _qkvg_proj(x, mod_lat, norm_g[i], w_lat, rope, tm=QKV_TILE)
            kc, vc = _ctx_kv_proj(ctx, mod_ctx, norm_g[i], w_kv.astype(BF16))
            x = _attn_layer(x, mod_lat, q, k, v, kc, vc, sg, attn_sink[j], w_out, tq=TOKEN_TILE)
    if depth % n_mixers != 1:
        raise NotImplementedError("final RMSNorm is fused into a trailing gMLP layer")
    return x
```
